```python
import math
import jax, jax.numpy as jnp
from jax import lax
import numpy as np

D_MODEL = 1024
BATCH = 8
SEQ = 8192
DEPTH = 4
DEC_BATCH = 2
DEC_SEQ = 8192
PAST_LEN = 128

GRID_W = 64
NA_HEADS = 8
NA_HEAD_DIM = 64
NA_W = NA_HEADS * NA_HEAD_DIM
NA_KH_MAX = 8
NA_KW = 16
NA_COL_BLOCK = NA_KW
NA_N_CB = GRID_W // NA_COL_BLOCK
NA_BAND = 2 * NA_KW
LRU_W = 512
LRU_BLOCKS = 8
LRU_BW = LRU_W // LRU_BLOCKS
CONV_W = 4
LRU_C = 8.0
CA_HEADS = 4
CA_HEAD_DIM = 128
CA_W = CA_HEADS * CA_HEAD_DIM
N_MEM = 256
D_FF = 2816
N_BRANCH = 3
IN_COLS = 3 * NA_W + 2 * LRU_W + CA_W
EPS = 1e-6
NEG = -1e30

kernel_name = "hybrid_na_rglru_memory_encoder"


def rmsnorm(x, g):
    xf = x.astype(jnp.float32)
    y = xf * lax.rsqrt(jnp.mean(xf * xf, axis=-1, keepdims=True) + EPS)
    return (y * g.astype(jnp.float32)).astype(x.dtype)


def swiglu(h, w_up, w_down):
    a, b = jnp.split(h @ w_up, 2, axis=-1)
    return (jax.nn.silu(a) * b) @ w_down


def _na_col_tables():
    j = np.arange(NA_N_CB)[:, None, None]
    qc = j * NA_COL_BLOCK + np.arange(NA_COL_BLOCK)[None, :, None]
    bs = np.clip(j * NA_COL_BLOCK - NA_KW // 2, 0, GRID_W - NA_BAND)
    kc = bs + np.arange(NA_BAND)[None, None, :]
    ws = np.clip(qc - NA_KW // 2, 0, GRID_W - NA_KW)
    valid = (kc >= ws) & (kc < ws + NA_KW)
    dc = np.clip(kc - qc, -(NA_KW - 1), NA_KW - 1) + (NA_KW - 1)
    band_idx = bs[:, 0, 0][:, None] + np.arange(NA_BAND)[None, :]
    return band_idx, valid, dc


def neighbourhood_attention(q, k, v, rpb):
    B, T, _ = q.shape
    rows = T // GRID_W
    kh = min(NA_KH_MAX, rows)
    scale = NA_HEAD_DIM ** -0.5

    def grid(a):
        return a.reshape(B, rows, GRID_W, NA_HEADS, NA_HEAD_DIM).transpose(0, 3, 1, 2, 4)

    qg, kg, vg = grid(q), grid(k), grid(v)
    band_idx, valid, dc = _na_col_tables()
    mask = jnp.asarray(valid)[:, :, None, :]
    dc_b = jnp.asarray(dc, dtype=jnp.int32)[:, :, None, :]

    def row_step(r):
        rs = jnp.clip(r - kh // 2, 0, rows - kh)
        k_rows = lax.dynamic_slice_in_dim(kg, rs, kh, axis=2)
        v_rows = lax.dynamic_slice_in_dim(vg, rs, kh, axis=2)
        k_band = k_rows[:, :, :, band_idx, :]
        v_band = v_rows[:, :, :, band_idx, :]
        q_row = lax.dynamic_index_in_dim(qg, r, axis=2, keepdims=False)
        q_row = q_row.reshape(B, NA_HEADS, NA_N_CB, NA_COL_BLOCK, NA_HEAD_DIM)
        s = jnp.einsum('bhjqd,bhkjcd->bhjqkc', q_row, k_band).astype(jnp.float32) * scale
        dr = rs + jnp.arange(kh, dtype=jnp.int32) - r + (NA_KH_MAX - 1)
        bias = rpb[:, dr[None, None, :, None], dc_b]
        s = jnp.where(mask, s + bias.astype(jnp.float32), NEG)
        p = jax.nn.softmax(s.reshape(s.shape[:4] + (kh * NA_BAND,)), axis=-1).reshape(s.shape)
        o = jnp.einsum('bhjqkc,bhkjcd->bhjqd', p.astype(v.dtype), v_band)
        return o.reshape(B, NA_HEADS, GRID_W, NA_HEAD_DIM)

    out = lax.map(row_step, jnp.arange(rows, dtype=jnp.int32))
    return out.transpose(1, 0, 3, 2, 4).reshape(B, T, NA_W)


def centred_conv(x, w, b):
    T = x.shape[1]
    left = CONV_W // 2
    xp = jnp.pad(x, ((0, 0), (left, CONV_W - 1 - left), (0, 0)))
    y = sum(xp[:, i:i + T, :] * w[i] for i in range(CONV_W))
    return y + b


def rg_lru(x, wa, ba, wi, bi, lam, reverse):
    B, T, _ = x.shape
    xb = x.reshape(B, T, LRU_BLOCKS, LRU_BW)
    f32 = jnp.float32
    r = jax.nn.sigmoid(jnp.einsum('btnc,ncd->btnd', xb, wa.astype(f32)).reshape(B, T, LRU_W) + ba.astype(f32))
    i = jax.nn.sigmoid(jnp.einsum('btnc,ncd->btnd', xb, wi.astype(f32)).reshape(B, T, LRU_W) + bi.astype(f32))
    log_a = -LRU_C * r * jax.nn.softplus(-lam.astype(f32))
    a = jnp.exp(log_a)
    u = jnp.sqrt(-jnp.expm1(2.0 * log_a)) * (i * x)

    def combine(e1, e2):
        a1, b1 = e1
        a2, b2 = e2
        return a1 * a2, a2 * b1 + b2

    _, h = lax.associative_scan(combine, (a, u), axis=1, reverse=reverse)
    return h


def memory_attention(q, mem_n, w_kv):
    B, T, _ = q.shape
    M = mem_n.shape[1]
    k, v = jnp.split(mem_n @ w_kv, 2, axis=-1)
    qh = q.reshape(B, T, CA_HEADS, CA_HEAD_DIM)
    kh = k.reshape(B, M, CA_HEADS, CA_HEAD_DIM)
    vh = v.reshape(B, M, CA_HEADS, CA_HEAD_DIM)
    s = jnp.einsum('bthd,bmhd->bhtm', qh, kh).astype(jnp.float32) * (CA_HEAD_DIM ** -0.5)
    p = jax.nn.softmax(s, axis=-1).astype(q.dtype)
    return jnp.einsum('bhtm,bmhd->bthd', p, vh).reshape(B, T, CA_W)


def _layer(x, mem, p, l):
    h = rmsnorm(x, p['g_ffn1_pre'][l])
    x = x + 0.5 * rmsnorm(swiglu(h, p['w_ffn1_up'][l], p['w_ffn1_down'][l]), p['g_ffn1_post'][l])

    h = rmsnorm(x, p['g_mix_pre'][l])
    proj = h @ p['w_in'][l]
    splits = [NA_W, 2 * NA_W, 3 * NA_W, 3 * NA_W + LRU_W, 3 * NA_W + 2 * LRU_W]
    q_na, k_na, v_na, x_lru, g_lru, q_ca = jnp.split(proj, splits, axis=-1)

    y_na = neighbourhood_attention(q_na, k_na, v_na, p['na_rpb'][l])

    xc = centred_conv(x_lru, p['conv_w'][l], p['conv_b'][l]).astype(jnp.float32)
    h_f = rg_lru(xc, p['lru_wa'][l, 0], p['lru_ba'][l, 0], p['lru_wi'][l, 0], p['lru_bi'][l, 0],
                 p['lru_lambda'][l, 0], reverse=False)
    h_b = rg_lru(xc, p['lru_wa'][l, 1], p['lru_ba'][l, 1], p['lru_wi'][l, 1], p['lru_bi'][l, 1],
                 p['lru_lambda'][l, 1], reverse=True)
    y_lru = ((h_f + h_b) * jax.nn.gelu(g_lru.astype(jnp.float32))).astype(x.dtype)

    y_ca = memory_attention(q_ca, rmsnorm(mem, p['g_mem'][l]), p['w_mem_kv'][l])

    gates = jax.nn.sigmoid((h @ p['w_gate'][l] + p['b_gate'][l]).astype(jnp.float32)).astype(x.dtype)
    g_na, g_lr, g_ca = jnp.split(gates, N_BRANCH, axis=-1)
    merged = (g_na * (y_na @ p['w_branch_na'][l])
              + g_lr * (y_lru @ p['w_branch_lru'][l])
              + g_ca * (y_ca @ p['w_branch_ca'][l]))
    x = x + rmsnorm(merged @ p['w_out'][l], p['g_mix_post'][l])

    h = rmsnorm(x, p['g_ffn2_pre'][l])
    x = x + 0.5 * rmsnorm(swiglu(h, p['w_ffn2_up'][l], p['w_ffn2_down'][l]), p['g_ffn2_post'][l])
    return x


def _trunk(x, mem, p):
    for l in range(DEPTH):
        x = _layer(x, mem, p, l)
    return x


def setup_inputs(seed: int = 0) -> dict:
    key = jax.random.key(seed)
    ks = jax.random.split(key, 40)
    f32 = jnp.float32

    def nrm(k, shape, fan_in):
        return jax.random.normal(k, shape, f32) * (fan_in ** -0.5)

    def gain(k, shape):
        return 1.0 + 0.05 * jax.random.normal(k, shape, f32)

    a0 = jax.random.uniform(ks[20], (DEPTH, 2, LRU_W), f32, 0.9, 0.999)
    sig = a0 ** (1.0 / LRU_C)
    lam = jnp.log(sig) - jnp.log1p(-sig)

    return {
        'x_prompt': jax.random.normal(ks[0], (BATCH, SEQ, D_MODEL), f32),
        'x_sample': jax.random.normal(ks[1], (DEC_BATCH, DEC_SEQ, D_MODEL), f32),
        'mem_prompt': jax.random.normal(ks[2], (BATCH, N_MEM, D_MODEL), f32),
        'mem_sample': jax.random.normal(ks[3], (DEC_BATCH, N_MEM, D_MODEL), f32),
        'g_ffn1_pre': gain(ks[4], (DEPTH, D_MODEL)),
        'w_ffn1_up': nrm(ks[5], (DEPTH, D_MODEL, 2 * D_FF), D_MODEL),
        'w_ffn1_down': nrm(ks[6], (DEPTH, D_FF, D_MODEL), D_FF),
        'g_ffn1_post': gain(ks[7], (DEPTH, D_MODEL)),
        'g_mix_pre': gain(ks[8], (DEPTH, D_MODEL)),
        'w_in': nrm(ks[9], (DEPTH, D_MODEL, IN_COLS), D_MODEL),
        'na_rpb': 0.1 * jax.random.normal(ks[10], (DEPTH, NA_HEADS, 2 * NA_KH_MAX - 1, 2 * NA_KW - 1), f32),
        'conv_w': nrm(ks[11], (DEPTH, CONV_W, LRU_W), CONV_W),
        'conv_b': 0.01 * jax.random.normal(ks[12], (DEPTH, LRU_W), f32),
        'lru_wa': nrm(ks[13], (DEPTH, 2, LRU_BLOCKS, LRU_BW, LRU_BW), LRU_BW),
        'lru_ba': 0.1 * jax.random.normal(ks[14], (DEPTH, 2, LRU_W), f32),
        'lru_wi': nrm(ks[15], (DEPTH, 2, LRU_BLOCKS, LRU_BW, LRU_BW), LRU_BW),
        'lru_bi': 0.1 * jax.random.normal(ks[16], (DEPTH, 2, LRU_W), f32),
        'lru_lambda': lam,
        'g_mem': gain(ks[17], (DEPTH, D_MODEL)),
        'w_mem_kv': nrm(ks[18], (DEPTH, D_MODEL, 2 * CA_W), D_MODEL),
        'w_gate': nrm(ks[19], (DEPTH, D_MODEL, N_BRANCH * D_MODEL), D_MODEL),
        'b_gate': 0.1 * jax.random.normal(ks[21], (DEPTH, N_BRANCH * D_MODEL), f32),
        'w_branch_na': nrm(ks[22], (DEPTH, NA_W, D_MODEL), NA_W),
        'w_branch_lru': nrm(ks[23], (DEPTH, LRU_W, D_MODEL), LRU_W),
        'w_branch_ca': nrm(ks[24], (DEPTH, CA_W, D_MODEL), CA_W),
        'w_out': nrm(ks[25], (DEPTH, D_MODEL, D_MODEL), D_MODEL),
        'g_mix_post': gain(ks[26], (DEPTH, D_MODEL)),
        'g_ffn2_pre': gain(ks[27], (DEPTH, D_MODEL)),
        'w_ffn2_up': nrm(ks[28], (DEPTH, D_MODEL, 2 * D_FF), D_MODEL),
        'w_ffn2_down': nrm(ks[29], (DEPTH, D_FF, D_MODEL), D_FF),
        'g_ffn2_post': gain(ks[30], (DEPTH, D_MODEL)),
    }


def reference(x_prompt, x_sample, mem_prompt, mem_sample,
              g_ffn1_pre, w_ffn1_up, w_ffn1_down, g_ffn1_post,
              g_mix_pre, w_in, na_rpb, conv_w, conv_b,
              lru_wa, lru_ba, lru_wi, lru_bi, lru_lambda,
              g_mem, w_mem_kv, w_gate, b_gate,
              w_branch_na, w_branch_lru, w_branch_ca, w_out, g_mix_post,
              g_ffn2_pre, w_ffn2_up, w_ffn2_down, g_ffn2_post):
    p = dict(
        g_ffn1_pre=g_ffn1_pre, w_ffn1_up=w_ffn1_up, w_ffn1_down=w_ffn1_down, g_ffn1_post=g_ffn1_post,
        g_mix_pre=g_mix_pre, w_in=w_in, na_rpb=na_rpb, conv_w=conv_w, conv_b=conv_b,
        lru_wa=lru_wa, lru_ba=lru_ba, lru_wi=lru_wi, lru_bi=lru_bi, lru_lambda=lru_lambda,
        g_mem=g_mem, w_mem_kv=w_mem_kv, w_gate=w_gate, b_gate=b_gate,
        w_branch_na=w_branch_na, w_branch_lru=w_branch_lru, w_branch_ca=w_branch_ca,
        w_out=w_out, g_mix_post=g_mix_post,
        g_ffn2_pre=g_ffn2_pre, w_ffn2_up=w_ffn2_up, w_ffn2_down=w_ffn2_down, g_ffn2_post=g_ffn2_post,
    )
    y_prompt = _trunk(x_prompt, mem_prompt, p)
    y_sample = _trunk(x_sample, mem_sample, p)
    return (y_prompt, y_sample)
```

```python
import functools

import numpy as np
import jax
import jax.numpy as jnp
from jax import lax
from jax.experimental import pallas as pl
from jax.experimental.pallas import tpu as pltpu

GRID_W = 64
NA_HEADS = 8
NA_HEAD_DIM = 64
NA_W = NA_HEADS * NA_HEAD_DIM
NA_KH = 8
NA_KW = 16
NA_N_CB = GRID_W // NA_KW
NA_BAND = 2 * NA_KW
LRU_W = 512
LRU_BLOCKS = 8
LRU_BW = LRU_W // LRU_BLOCKS
CONV_W = 4
LRU_C = 8.0
CA_HEADS = 4
CA_HEAD_DIM = 128
CA_W = CA_HEADS * CA_HEAD_DIM
EPS = 1e-6
NEG = -1e30

LANES = 128
SUBLANES = 8
VMEM_LIMIT_BYTES = 56 * 1024 * 1024

TOKEN_TILE = 512
FF_CHUNKS = 2
NA_QROWS = 8
NA_KROWS = 2 * NA_KH
NA_HEADS_PER_STEP = LANES // NA_HEAD_DIM
LRU_CHUNK = 512
LRU_GROUP = LANES // LRU_BW

BF16 = jnp.bfloat16
F32 = jnp.float32


def _rms(x, g):
    return x * lax.rsqrt(jnp.mean(x * x, axis=-1, keepdims=True) + EPS) * g


def _dot(a, b):
    return jnp.dot(a, b, preferred_element_type=F32)


def _dot_nt(a, b):
    return lax.dot_general(a, b, (((1,), (1,)), ((), ())), preferred_element_type=F32)


def _swiglu_half_step(x, g_pre, w_up_ref, w_down_ref, g_post):
    d_ff = w_down_ref.shape[0]
    cw = d_ff // FF_CHUNKS
    h = _rms(x, g_pre).astype(BF16)
    acc = None
    for c in range(FF_CHUNKS):
        a = _dot(h, w_up_ref[:, c * cw:(c + 1) * cw])
        b = _dot(h, w_up_ref[:, d_ff + c * cw:d_ff + (c + 1) * cw])
        act = (a * jax.nn.sigmoid(a) * b).astype(BF16)
        part = _dot(act, w_down_ref[c * cw:(c + 1) * cw, :])
        acc = part if acc is None else acc + part
    return x + 0.5 * _rms(acc, g_post)


def _band_start(j):
    return int(np.clip(j * NA_KW - NA_KW // 2, 0, GRID_W - NA_BAND))


def _ffn_proj_kernel(x_ref, g1_ref, wup_ref, wdn_ref, g2_ref, gm_ref, win_ref,
                     x1_ref, qcb_ref, kband_ref, vband_ref, xlru_ref, glru_ref, qca_ref):
    x1 = _swiglu_half_step(x_ref[0], g1_ref[...], wup_ref, wdn_ref, g2_ref[...])
    x1_ref[0] = x1
    h = _rms(x1, gm_ref[...]).astype(BF16)
    proj = _dot(h, win_ref[...])
    tm = proj.shape[0]
    rows = tm // GRID_W
    q = (proj[:, 0:NA_W] * (NA_HEAD_DIM ** -0.5)).reshape(rows, GRID_W, NA_W)
    k = proj[:, NA_W:2 * NA_W].reshape(rows, GRID_W, NA_W)
    v = proj[:, 2 * NA_W:3 * NA_W].reshape(rows, GRID_W, NA_W)
    for j in range(NA_N_CB):
        bs = _band_start(j)
        qcb_ref[0, j] = q[:, j * NA_KW:(j + 1) * NA_KW, :].reshape(rows * NA_KW, NA_W).astype(BF16)
        kband_ref[0, j] = k[:, bs:bs + NA_BAND, :].reshape(rows * NA_BAND, NA_W).astype(BF16)
        vband_ref[0, j] = v[:, bs:bs + NA_BAND, :].reshape(rows * NA_BAND, NA_W).astype(BF16)
    o = 3 * NA_W
    xlru_ref[0] = proj[:, o:o + LRU_W]
    glru_ref[0] = proj[:, o + LRU_W:o + 2 * LRU_W]
    qca_ref[0] = proj[:, o + 2 * LRU_W:o + 2 * LRU_W + CA_W].astype(BF16)


def _resident(shape, layer):
    nd = len(shape)
    return pl.BlockSpec((None,) + tuple(shape), lambda *_: (layer,) + (0,) * nd,
                        pipeline_mode=pl.Buffered(1))


def _ffn_proj_call(layer, x, g1, wup, wdn, g2, gm, win):
    B, T, D = x.shape
    tm = TOKEN_TILE
    tok = lambda w: pl.BlockSpec((1, tm, w), lambda b, i: (b, i, 0))
    out_shape = (
        jax.ShapeDtypeStruct((B, T, D), F32),
        jax.ShapeDtypeStruct((B, NA_N_CB, T // NA_N_CB, NA_W), BF16),
        jax.ShapeDtypeStruct((B, NA_N_CB, T // 2, NA_W), BF16),
        jax.ShapeDtypeStruct((B, NA_N_CB, T // 2, NA_W), BF16),
        jax.ShapeDtypeStruct((B, T, LRU_W), F32),
        jax.ShapeDtypeStruct((B, T, LRU_W), F32),
        jax.ShapeDtypeStruct((B, T, CA_W), BF16),
    )
    out_specs = (
        tok(D),
        pl.BlockSpec((1, NA_N_CB, tm // NA_N_CB, NA_W), lambda b, i: (b, 0, i, 0)),
        pl.BlockSpec((1, NA_N_CB, tm // 2, NA_W), lambda b, i: (b, 0, i, 0)),
        pl.BlockSpec((1, NA_N_CB, tm // 2, NA_W), lambda b, i: (b, 0, i, 0)),
        tok(LRU_W), tok(LRU_W), tok(CA_W),
    )
    in_specs = [tok(D)] + [_resident(w.shape[1:], layer) for w in (g1, wup, wdn, g2, gm, win)]
    return pl.pallas_call(
        _ffn_proj_kernel,
        grid=(B, T // tm),
        in_specs=in_specs,
        out_specs=out_specs,
        out_shape=out_shape,
        compiler_params=pltpu.CompilerParams(
            dimension_semantics=("arbitrary", "arbitrary"), vmem_limit_bytes=VMEM_LIMIT_BYTES),
        name="ffn_proj",
    )(x, g1, wup, wdn, g2, gm, win)


def _mem_kv_kernel(mem_ref, g_ref, w_ref, kv_ref):
    h = _rms(mem_ref[0], g_ref[...]).astype(BF16)
    kv_ref[0, 0] = _dot(h, w_ref[...]).astype(BF16)


def _mem_kv_call(mem, g_mem, w_kv):
    B, M, D = mem.shape
    L = w_kv.shape[0]
    return pl.pallas_call(
        _mem_kv_kernel,
        grid=(L, B),
        in_specs=[
            pl.BlockSpec((1, M, D), lambda l, b: (b, 0, 0)),
            pl.BlockSpec((None, 1, D), lambda l, b: (l, 0, 0)),
            pl.BlockSpec((None, D, 2 * CA_W), lambda l, b: (l, 0, 0)),
        ],
        out_specs=pl.BlockSpec((1, 1, M, 2 * CA_W), lambda l, b: (l, b, 0, 0)),
        out_shape=jax.ShapeDtypeStruct((L, B, M, 2 * CA_W), BF16),
        compiler_params=pltpu.CompilerParams(dimension_semantics=("arbitrary", "arbitrary")),
        name="mem_kv",
    )(mem, g_mem, w_kv)


def _na_key_row_start(i, rows):
    return jnp.clip(i * NA_QROWS - NA_KH // 2, 0, rows - NA_KROWS)


def _na_bias_tables(rpb, rows):
    n_blocks = rows // NA_QROWS
    a = np.arange(NA_QROWS)[:, None]
    e = np.arange(NA_KROWS)[None, :]
    dr_tabs, row_ok = [], []
    for i in (0, min(1, n_blocks - 1), n_blocks - 1):
        ks = int(np.clip(i * NA_QROWS - NA_KH // 2, 0, rows - NA_KROWS))
        r = i * NA_QROWS + a
        kr = ks + e
        rs = np.clip(r - NA_KH // 2, 0, rows - NA_KH)
        row_ok.append((kr >= rs) & (kr < rs + NA_KH))
        dr_tabs.append(np.clip(kr - r + NA_KH - 1, 0, 2 * NA_KH - 2))
    dr_tab = np.stack(dr_tabs)
    row_ok = np.stack(row_ok)
    j = np.arange(NA_N_CB)[:, None, None]
    qc = j * NA_KW + np.arange(NA_KW)[None, :, None]
    bs = np.clip(j * NA_KW - NA_KW // 2, 0, GRID_W - NA_BAND)
    kc = bs + np.arange(NA_BAND)[None, None, :]
    ws = np.clip(qc - NA_KW // 2, 0, GRID_W - NA_KW)
    col_ok = (kc >= ws) & (kc < ws + NA_KW)
    dc_tab = np.clip(kc - qc, -(NA_KW - 1), NA_KW - 1) + (NA_KW - 1)
    dr_i = dr_tab[:, None, :, None, :, None]
    dc_i = dc_tab[None, :, None, :, None, :]
    ok = row_ok[:, None, :, None, :, None] & col_ok[None, :, None, :, None, :]
    shape = np.broadcast_shapes(dr_i.shape, dc_i.shape)
    dr_i = np.broadcast_to(dr_i, shape)
    dc_i = np.broadcast_to(dc_i, shape)
    bias = rpb[:, :, dr_i, dc_i]
    bias = jnp.where(jnp.asarray(ok), bias, NEG)
    bias = jnp.moveaxis(bias, 2, 1)
    L, H = rpb.shape[:2]
    return bias.reshape(L, 3, H, NA_N_CB, NA_QROWS * NA_KW, NA_KROWS * NA_BAND).astype(F32)


def _na_kernel(q_ref, k_ref, v_ref, bias_ref, o_ref, *, rows):
    i = pl.program_id(2)
    nq = NA_QROWS * NA_KW
    nk = NA_KROWS * NA_BAND
    ks = pl.multiple_of(_na_key_row_start(i, rows) * NA_BAND, NA_BAND * (NA_KH // 2))
    lane = lax.broadcasted_iota(jnp.int32, (nq, LANES), 1)
    first_head = lane < NA_HEAD_DIM
    for j in range(NA_N_CB):
        q = q_ref[0, j]
        k = k_ref[0, j, pl.ds(ks, nk), :]
        v = v_ref[0, j, pl.ds(ks, nk), :]
        outs = []
        for hh in range(NA_HEADS_PER_STEP):
            keep = first_head if hh == 0 else jnp.logical_not(first_head)
            qh = jnp.where(keep, q, jnp.zeros_like(q))
            s = _dot_nt(qh, k) + bias_ref[0, hh, j]
            m = jnp.max(s, axis=-1, keepdims=True)
            p = jnp.exp(s - m)
            l = jnp.sum(p, axis=-1, keepdims=True)
            outs.append(_dot(p.astype(BF16), v) / l)
        o = jnp.where(first_head, outs[0], outs[1]).astype(BF16)
        for a in range(NA_QROWS):
            o_ref[0, pl.ds(a * GRID_W + j * NA_KW, NA_KW), :] = o[a * NA_KW:(a + 1) * NA_KW, :]


def _na_call(layer, q_cb, k_band, v_band, bias):
    B, _, tq, _ = q_cb.shape
    T = tq * NA_N_CB
    rows = T // GRID_W
    n_blocks = rows // NA_QROWS
    n_pairs = NA_HEADS // NA_HEADS_PER_STEP
    nq = NA_QROWS * NA_KW
    nk = NA_KROWS * NA_BAND

    def variant(i):
        return jnp.where(i == 0, 0, jnp.where(i == n_blocks - 1, 2, 1))

    kv_spec = pl.BlockSpec((1, NA_N_CB, T // 2, LANES), lambda b, hp, i: (b, 0, 0, hp))
    return pl.pallas_call(
        functools.partial(_na_kernel, rows=rows),
        grid=(B, n_pairs, n_blocks),
        in_specs=[
            pl.BlockSpec((1, NA_N_CB, nq, LANES), lambda b, hp, i: (b, 0, i, hp)),
            kv_spec, kv_spec,
            pl.BlockSpec((None, 1, NA_HEADS_PER_STEP, NA_N_CB, nq, nk),
                         lambda b, hp, i: (layer, variant(i), hp, 0, 0, 0)),
        ],
        out_specs=pl.BlockSpec((1, NA_QROWS * GRID_W, LANES), lambda b, hp, i: (b, i, hp)),
        out_shape=jax.ShapeDtypeStruct((B, T, NA_W), BF16),
        compiler_params=pltpu.CompilerParams(
            dimension_semantics=("arbitrary", "arbitrary", "arbitrary"), vmem_limit_bytes=VMEM_LIMIT_BYTES),
        name="na_attn",
    )(q_cb, k_band, v_band, bias)


def _lru_gate_weights(wa, wi):
    L = wa.shape[0]
    ng = LRU_BLOCKS // LRU_GROUP

    def blockdiag(w):
        w = w.reshape(L, 2, ng, LRU_GROUP, LRU_BW, LRU_BW)
        eye = jnp.eye(LRU_GROUP, dtype=w.dtype)
        full = w[:, :, :, :, :, None, :] * eye[:, None, :, None]
        return full.reshape(L, 2, ng, LANES, LANES)

    return jnp.concatenate([blockdiag(wa), blockdiag(wi)], axis=-1).astype(BF16)


def _lru_gate_bias(ba, bi):
    L = ba.shape[0]
    ng = LRU_W // LANES
    return jnp.concatenate([ba.reshape(L, 2, ng, 1, LANES), bi.reshape(L, 2, ng, 1, LANES)], axis=-1)


def _softplus(x):
    return jnp.maximum(x, 0.0) + jnp.log1p(jnp.exp(-jnp.abs(x)))


def _lru_inputs(prev_ref, x_ref, next_ref, has_prev, has_next, cw_ref, cb_ref, w_ref, b_ref, lam_ref, d,
                a_scr, u_scr):
    ct = x_ref.shape[1]
    prev = jnp.where(has_prev, prev_ref[0], 0.0)
    nxt = jnp.where(has_next, next_ref[0], 0.0)
    xe = jnp.concatenate([prev, x_ref[0], nxt], axis=0)
    left = CONV_W // 2
    xc = cb_ref[...]
    for t in range(CONV_W):
        off = SUBLANES - left + t
        xc = xc + xe[off:off + ct, :] * cw_ref[t:t + 1, :]
    xcb = xc.astype(BF16)
    sp = _softplus(-lam_ref[d:d + 1, :])
    for g in range(LRU_W // LANES):
        sl = slice(g * LANES, (g + 1) * LANES)
        z = _dot(xcb[:, sl], w_ref[d, g]) + b_ref[d, g]
        r = jax.nn.sigmoid(z[:, :LANES])
        gi = jax.nn.sigmoid(z[:, LANES:])
        log_a = (-LRU_C) * r * sp[:, sl]
        a = jnp.exp(log_a)
        u = jnp.sqrt(-jnp.tanh(log_a) * (a * a + 1.0)) * (gi * xc[:, sl])
        a_scr[:, sl] = a
        u_scr[:, sl] = u


def _slab_scan(a, b, reverse):
    row = lax.broadcasted_iota(jnp.int32, a.shape, 0)
    for sh in (1, 2, 4):
        if reverse:
            ok = row < SUBLANES - sh
            a_sh = pltpu.roll(a, SUBLANES - sh, 0)
            b_sh = pltpu.roll(b, SUBLANES - sh, 0)
        else:
            ok = row >= sh
            a_sh = pltpu.roll(a, sh, 0)
            b_sh = pltpu.roll(b, sh, 0)
        b = a * jnp.where(ok, b_sh, 0.0) + b
        a = a * jnp.where(ok, a_sh, 1.0)
    return a, b


def _lru_kernel(xf_ref, pf_ref, nf_ref, xb_ref, pb_ref, nb_ref, cw_ref, cb_ref, w_ref, b_ref, lam_ref,
                hf_ref, hb_ref, af_scr, uf_scr, ab_scr, ub_scr, cf_scr, cbk_scr):
    c = pl.program_id(1)
    nc = pl.num_programs(1)
    ct = xf_ref.shape[1]
    n_slabs = ct // SUBLANES

    @pl.when(c == 0)
    def _():
        cf_scr[...] = jnp.zeros_like(cf_scr)
        cbk_scr[...] = jnp.zeros_like(cbk_scr)

    _lru_inputs(pf_ref, xf_ref, nf_ref, c > 0, c < nc - 1, cw_ref, cb_ref, w_ref, b_ref, lam_ref, 0,
                af_scr, uf_scr)
    _lru_inputs(pb_ref, xb_ref, nb_ref, c < nc - 1, c > 0, cw_ref, cb_ref, w_ref, b_ref, lam_ref, 1,
                ab_scr, ub_scr)

    def body(s, carry):
        cf, cb = carry
        of = pl.multiple_of(s * SUBLANES, SUBLANES)
        a, b = _slab_scan(af_scr[pl.ds(of, SUBLANES), :], uf_scr[pl.ds(of, SUBLANES), :], False)
        h = b + a * cf
        hf_ref[0, pl.ds(of, SUBLANES), :] = h
        cf = h[SUBLANES - 1:SUBLANES, :]
        ob = pl.multiple_of((n_slabs - 1 - s) * SUBLANES, SUBLANES)
        a, b = _slab_scan(ab_scr[pl.ds(ob, SUBLANES), :], ub_scr[pl.ds(ob, SUBLANES), :], True)
        h = b + a * cb
        hb_ref[0, pl.ds(ob, SUBLANES), :] = h
        cb = h[0:1, :]
        return cf, cb

    cf, cb = lax.fori_loop(0, n_slabs, body, (cf_scr[...], cbk_scr[...]), unroll=2)
    cf_scr[...] = cf
    cbk_scr[...] = cb


def _lru_call(layer, x_lru, conv_w, conv_b, w_gate, b_gate, lam):
    B, T, W = x_lru.shape
    ct = LRU_CHUNK
    nc = T // ct
    hb = ct // SUBLANES
    last_halo = T // SUBLANES - 1

    main_f = pl.BlockSpec((1, ct, W), lambda b, c: (b, c, 0))
    prev_f = pl.BlockSpec((1, SUBLANES, W), lambda b, c: (b, jnp.maximum(c * hb - 1, 0), 0))
    next_f = pl.BlockSpec((1, SUBLANES, W), lambda b, c: (b, jnp.minimum((c + 1) * hb, last_halo), 0))
    main_b = pl.BlockSpec((1, ct, W), lambda b, c: (b, nc - 1 - c, 0))
    prev_b = pl.BlockSpec((1, SUBLANES, W), lambda b, c: (b, jnp.maximum((nc - 1 - c) * hb - 1, 0), 0))
    next_b = pl.BlockSpec((1, SUBLANES, W), lambda b, c: (b, jnp.minimum((nc - c) * hb, last_halo), 0))
    params = (conv_w, conv_b, w_gate, b_gate, lam)
    return pl.pallas_call(
        _lru_kernel,
        grid=(B, nc),
        in_specs=[main_f, prev_f, next_f, main_b, prev_b, next_b] + [_resident(p.shape[1:], layer) for p in params],
        out_specs=(main_f, main_b),
        out_shape=(jax.ShapeDtypeStruct((B, T, W), F32), jax.ShapeDtypeStruct((B, T, W), F32)),
        scratch_shapes=[pltpu.VMEM((ct, W), F32)] * 4 + [pltpu.VMEM((1, W), F32)] * 2,
        compiler_params=pltpu.CompilerParams(dimension_semantics=("arbitrary", "arbitrary")),
        name="rg_lru",
    )(x_lru, x_lru, x_lru, x_lru, x_lru, x_lru, *params)


def _merge_ffn_kernel(x_ref, yna_ref, hf_ref, hb_ref, glru_ref, qca_ref, kv_ref,
                      gm_ref, wg_ref, bg_ref, wna_ref, wlru_ref, wca_ref, wout_ref, gpost_ref,
                      g1_ref, wup_ref, wdn_ref, g2_ref, o_ref):
    x1 = x_ref[0]
    d = x1.shape[-1]
    h = _rms(x1, gm_ref[...]).astype(BF16)

    y_lru = ((hf_ref[0] + hb_ref[0]) * jax.nn.gelu(glru_ref[0])).astype(BF16)

    q = qca_ref[0]
    heads = []
    for hd in range(CA_HEADS):
        sl = slice(hd * CA_HEAD_DIM, (hd + 1) * CA_HEAD_DIM)
        k = kv_ref[0, 0, :, hd * CA_HEAD_DIM:(hd + 1) * CA_HEAD_DIM]
        v = kv_ref[0, 0, :, CA_W + hd * CA_HEAD_DIM:CA_W + (hd + 1) * CA_HEAD_DIM]
        s = _dot_nt(q[:, sl], k) * (CA_HEAD_DIM ** -0.5)
        m = jnp.max(s, axis=-1, keepdims=True)
        p = jnp.exp(s - m)
        l = jnp.sum(p, axis=-1, keepdims=True)
        heads.append((_dot(p.astype(BF16), v) / l).astype(BF16))
    y_ca = jnp.concatenate(heads, axis=-1)

    def gate(n):
        return jax.nn.sigmoid(_dot(h, wg_ref[:, n * d:(n + 1) * d]) + bg_ref[:, n * d:(n + 1) * d])

    merged = gate(0) * _dot(yna_ref[0], wna_ref[...])
    merged = merged + gate(1) * _dot(y_lru, wlru_ref[...])
    merged = merged + gate(2) * _dot(y_ca, wca_ref[...])
    x2 = x1 + _rms(_dot(merged.astype(BF16), wout_ref[...]), gpost_ref[...])
    o_ref[0] = _swiglu_half_step(x2, g1_ref[...], wup_ref, wdn_ref, g2_ref[...])


def _merge_ffn_call(layer, x1, y_na, h_f, h_b, g_lru, q_ca, kv, params):
    B, T, D = x1.shape
    M = kv.shape[2]
    tm = TOKEN_TILE
    tok = lambda w: pl.BlockSpec((1, tm, w), lambda b, i: (b, i, 0))
    in_specs = [tok(D), tok(NA_W), tok(LRU_W), tok(LRU_W), tok(LRU_W), tok(CA_W),
                pl.BlockSpec((1, 1, M, 2 * CA_W), lambda b, i: (layer, b, 0, 0))]
    in_specs += [_resident(p.shape[1:], layer) for p in params]
    return pl.pallas_call(
        _merge_ffn_kernel,
        grid=(B, T // tm),
        in_specs=in_specs,
        out_specs=tok(D),
        out_shape=jax.ShapeDtypeStruct((B, T, D), F32),
        compiler_params=pltpu.CompilerParams(
            dimension_semantics=("arbitrary", "arbitrary"), vmem_limit_bytes=VMEM_LIMIT_BYTES),
        name="merge_ffn",
    )(x1, y_na, h_f, h_b, g_lru, q_ca, kv, *params)


def kernel(x_prompt, x_sample, mem_prompt, mem_sample, g_ffn1_pre, w_ffn1_up, w_ffn1_down, g_ffn1_post, g_mix_pre, w_in, na_rpb, conv_w, conv_b, lru_wa, lru_ba, lru_wi, lru_bi, lru_lambda, g_mem, w_mem_kv, w_gate, b_gate, w_branch_na, w_branch_lru, w_branch_ca, w_out, g_mix_post, g_ffn2_pre, w_ffn2_up, w_ffn2_down, g_ffn2_post):
    n_prompt = x_prompt.shape[0]
    assert x_prompt.shape[1:] == x_sample.shape[1:] and mem_prompt.shape[1:] == mem_sample.shape[1:]
    x = jnp.concatenate([x_prompt, x_sample], axis=0)
    mem = jnp.concatenate([mem_prompt, mem_sample], axis=0)
    B, T, D = x.shape
    L = w_in.shape[0]
    rows = T // GRID_W
    assert T % (NA_QROWS * GRID_W) == 0 and rows >= NA_KROWS and T % LRU_CHUNK == 0 and T % TOKEN_TILE == 0
    assert w_ffn1_down.shape[1] % (FF_CHUNKS * LANES) == 0

    bf = lambda w: w.astype(BF16)
    vec = lambda g: g.reshape(L, 1, g.shape[-1])
    w1u, w1d, w2u, w2d = bf(w_ffn1_up), bf(w_ffn1_down), bf(w_ffn2_up), bf(w_ffn2_down)
    win, wg, wout, wkv = bf(w_in), bf(w_gate), bf(w_out), bf(w_mem_kv)
    wna, wlru, wca = bf(w_branch_na), bf(w_branch_lru), bf(w_branch_ca)
    lru_w = _lru_gate_weights(lru_wa, lru_wi)
    lru_b = _lru_gate_bias(lru_ba, lru_bi)
    bias = _na_bias_tables(na_rpb, rows)
    kv = _mem_kv_call(mem, vec(g_mem), wkv)

    for l in range(L):
        x1, q_cb, k_band, v_band, x_lru, g_lru, q_ca = _ffn_proj_call(
            l, x, vec(g_ffn1_pre), w1u, w1d, vec(g_ffn1_post), vec(g_mix_pre), win)
        y_na = _na_call(l, q_cb, k_band, v_band, bias)
        h_f, h_b = _lru_call(l, x_lru, conv_w, vec(conv_b), lru_w, lru_b, lru_lambda)
        x = _merge_ffn_call(
            l, x1, y_na, h_f, h_b, g_lru, q_ca, kv,
            (vec(g_mix_pre), wg, vec(b_gate), wna, wlru, wca, wout, vec(g_mix_post),
             vec(g_ffn2_pre), w2u, w2d, vec(g_ffn2_post)))
    return x[:n_prompt], x[n_prompt:]
```

```python
import functools

import numpy as np
import jax
import jax.numpy as jnp
from jax import lax
from jax.experimental import pallas as pl
from jax.experimental.pallas import tpu as pltpu

GRID_W = 64
NA_HEADS = 8
NA_HEAD_DIM = 64
NA_W = NA_HEADS * NA_HEAD_DIM
NA_KH = 8
NA_KW = 16
NA_N_CB = GRID_W // NA_KW
NA_BAND = 2 * NA_KW
LRU_W = 512
LRU_BLOCKS = 8
LRU_BW = LRU_W // LRU_BLOCKS
CONV_W = 4
LRU_C = 8.0
CA_HEADS = 4
CA_HEAD_DIM = 128
CA_W = CA_HEADS * CA_HEAD_DIM
EPS = 1e-6
NEG = -1e30
F32_TINY = float(np.finfo(np.float32).tiny)

LANES = 128
SUBLANES = 8
MXU_DIM = 256
VMEM_LIMIT_BYTES = 56 * 1024 * 1024

TOKEN_TILE = 512
FF_CHUNK_TILES = 6
NA_QROWS = 8
NA_KROWS = 2 * NA_KH
NA_HEADS_PER_STEP = LANES // NA_HEAD_DIM
NA_BLOCKS_PER_STEP = 4
LRU_CHUNK = 512
LRU_GROUP = LANES // LRU_BW

BF16 = jnp.bfloat16
F32 = jnp.float32


def _rms(x, g):
    return x * lax.rsqrt(jnp.mean(x * x, axis=-1, keepdims=True) + EPS) * g


def _dot(a, b):
    return jnp.dot(a, b, preferred_element_type=F32)


def _dot_nt(a, b):
    return lax.dot_general(a, b, (((1,), (1,)), ((), ())), preferred_element_type=F32)


def _ff_chunks(d_ff):
    tiles = d_ff // MXU_DIM
    widths = []
    while tiles > 0:
        n = min(FF_CHUNK_TILES, tiles)
        widths.append(n * MXU_DIM)
        tiles -= n
    return widths


def _swiglu_half_step(x, g_pre, w_up_ref, w_down_ref, g_post):
    d_ff = w_down_ref.shape[0]
    h = _rms(x, g_pre).astype(BF16)
    acc = None
    lo = 0
    for cw in _ff_chunks(d_ff):
        a = _dot(h, w_up_ref[:, lo:lo + cw])
        b = _dot(h, w_up_ref[:, d_ff + lo:d_ff + lo + cw])
        act = (a * jax.nn.sigmoid(a) * b).astype(BF16)
        part = _dot(act, w_down_ref[lo:lo + cw, :])
        acc = part if acc is None else acc + part
        lo += cw
    return x + 0.5 * _rms(acc, g_post)


def _band_start(j):
    return int(np.clip(j * NA_KW - NA_KW // 2, 0, GRID_W - NA_BAND))


def _ffn_proj_kernel(x_ref, g1_ref, wup_ref, wdn_ref, g2_ref, gm_ref, win_ref,
                     x1_ref, qcb_ref, kband_ref, vband_ref, xlru_ref, glru_ref, qca_ref):
    x1 = _swiglu_half_step(x_ref[0], g1_ref[...], wup_ref, wdn_ref, g2_ref[...])
    x1_ref[0] = x1
    h = _rms(x1, gm_ref[...]).astype(BF16)
    proj = _dot(h, win_ref[...])
    tm = proj.shape[0]
    rows = tm // GRID_W
    q = (proj[:, 0:NA_W] * (NA_HEAD_DIM ** -0.5)).reshape(rows, GRID_W, NA_W)
    k = proj[:, NA_W:2 * NA_W].reshape(rows, GRID_W, NA_W)
    v = proj[:, 2 * NA_W:3 * NA_W].reshape(rows, GRID_W, NA_W)
    for j in range(NA_N_CB):
        bs = _band_start(j)
        qcb_ref[0, j] = q[:, j * NA_KW:(j + 1) * NA_KW, :].reshape(rows * NA_KW, NA_W).astype(BF16)
        kband_ref[0, j] = k[:, bs:bs + NA_BAND, :].reshape(rows * NA_BAND, NA_W).astype(BF16)
        vband_ref[0, j] = v[:, bs:bs + NA_BAND, :].reshape(rows * NA_BAND, NA_W).astype(BF16)
    o = 3 * NA_W
    xlru_ref[0] = proj[:, o:o + LRU_W]
    glru_ref[0] = proj[:, o + LRU_W:o + 2 * LRU_W]
    qca_ref[0] = proj[:, o + 2 * LRU_W:o + 2 * LRU_W + CA_W].astype(BF16)


def _resident(shape, layer):
    nd = len(shape)
    return pl.BlockSpec((None,) + tuple(shape), lambda *_: (layer,) + (0,) * nd,
                        pipeline_mode=pl.Buffered(1))


def _ffn_proj_call(layer, x, g1, wup, wdn, g2, gm, win):
    B, T, D = x.shape
    tm = TOKEN_TILE
    tok = lambda w: pl.BlockSpec((1, tm, w), lambda b, i: (b, i, 0))
    out_shape = (
        jax.ShapeDtypeStruct((B, T, D), F32),
        jax.ShapeDtypeStruct((B, NA_N_CB, T // NA_N_CB, NA_W), BF16),
        jax.ShapeDtypeStruct((B, NA_N_CB, T // 2, NA_W), BF16),
        jax.ShapeDtypeStruct((B, NA_N_CB, T // 2, NA_W), BF16),
        jax.ShapeDtypeStruct((B, T, LRU_W), F32),
        jax.ShapeDtypeStruct((B, T, LRU_W), F32),
        jax.ShapeDtypeStruct((B, T, CA_W), BF16),
    )
    out_specs = (
        tok(D),
        pl.BlockSpec((1, NA_N_CB, tm // NA_N_CB, NA_W), lambda b, i: (b, 0, i, 0)),
        pl.BlockSpec((1, NA_N_CB, tm // 2, NA_W), lambda b, i: (b, 0, i, 0)),
        pl.BlockSpec((1, NA_N_CB, tm // 2, NA_W), lambda b, i: (b, 0, i, 0)),
        tok(LRU_W), tok(LRU_W), tok(CA_W),
    )
    in_specs = [tok(D)] + [_resident(w.shape[1:], layer) for w in (g1, wup, wdn, g2, gm, win)]
    return pl.pallas_call(
        _ffn_proj_kernel,
        grid=(B, T // tm),
        in_specs=in_specs,
        out_specs=out_specs,
        out_shape=out_shape,
        compiler_params=pltpu.CompilerParams(
            dimension_semantics=("arbitrary", "arbitrary"), vmem_limit_bytes=VMEM_LIMIT_BYTES),
        name="ffn_proj",
    )(x, g1, wup, wdn, g2, gm, win)


def _mem_kv_kernel(mem_ref, g_ref, w_ref, kv_ref):
    h = _rms(mem_ref[0], g_ref[...]).astype(BF16)
    kv_ref[0, 0] = _dot(h, w_ref[...]).astype(BF16)


def _mem_kv_call(mem, g_mem, w_kv):
    B, M, D = mem.shape
    L = w_kv.shape[0]
    return pl.pallas_call(
        _mem_kv_kernel,
        grid=(L, B),
        in_specs=[
            pl.BlockSpec((1, M, D), lambda l, b: (b, 0, 0)),
            pl.BlockSpec((None, 1, D), lambda l, b: (l, 0, 0)),
            pl.BlockSpec((None, D, 2 * CA_W), lambda l, b: (l, 0, 0)),
        ],
        out_specs=pl.BlockSpec((1, 1, M, 2 * CA_W), lambda l, b: (l, b, 0, 0)),
        out_shape=jax.ShapeDtypeStruct((L, B, M, 2 * CA_W), BF16),
        compiler_params=pltpu.CompilerParams(dimension_semantics=("arbitrary", "arbitrary")),
        name="mem_kv",
    )(mem, g_mem, w_kv)


def _na_key_row_start(i, rows):
    return jnp.clip(i * NA_QROWS - NA_KH // 2, 0, rows - NA_KROWS)


def _na_bias_tables(rpb, rows):
    L, H = rpb.shape[:2]
    n_blocks = rows // NA_QROWS
    n_dr = 2 * NA_KH - 1
    j = np.arange(NA_N_CB)[:, None, None]
    qc = j * NA_KW + np.arange(NA_KW)[None, :, None]
    bs = np.clip(j * NA_KW - NA_KW // 2, 0, GRID_W - NA_BAND)
    kc = bs + np.arange(NA_BAND)[None, None, :]
    ws = np.clip(qc - NA_KW // 2, 0, GRID_W - NA_KW)
    col_ok = (kc >= ws) & (kc < ws + NA_KW)
    rp = jnp.pad(rpb, ((0, 0), (0, 0), (0, 0), (NA_BAND, NA_BAND)))
    cols = []
    for jj in range(NA_N_CB):
        for q in range(NA_KW):
            shift = NA_BAND + int(bs[jj, 0, 0]) - (jj * NA_KW + q) + (NA_KW - 1)
            cols.append(rp[..., shift:shift + NA_BAND])
    t1 = jnp.stack(cols).reshape(NA_N_CB, NA_KW, L, H, n_dr, NA_BAND)
    t1 = jnp.where(jnp.asarray(col_ok)[:, :, None, None, None, :], t1, NEG)
    t1 = jnp.pad(t1, ((0, 0),) * 4 + ((NA_KROWS, NA_KROWS), (0, 0)), constant_values=NEG)
    a = np.arange(NA_QROWS)[:, None]
    e = np.arange(NA_KROWS)[None, :]
    runs, row_ok = [], []
    for i in (0, min(1, n_blocks - 1), n_blocks - 1):
        ks = int(np.clip(i * NA_QROWS - NA_KH // 2, 0, rows - NA_KROWS))
        r = i * NA_QROWS + a
        rs = np.clip(r - NA_KH // 2, 0, rows - NA_KH)
        row_ok.append((ks + e >= rs) & (ks + e < rs + NA_KH))
        for aa in range(NA_QROWS):
            start = NA_KROWS + ks - (i * NA_QROWS + aa) + (NA_KH - 1)
            runs.append(t1[:, :, :, :, start:start + NA_KROWS, :])
    bias = jnp.stack(runs).reshape(3, NA_QROWS, NA_N_CB, NA_KW, L, H, NA_KROWS, NA_BAND)
    row_ok = np.stack(row_ok)
    bias = jnp.where(jnp.asarray(row_ok)[:, :, None, None, None, None, :, None], bias, NEG)
    bias = jnp.transpose(bias, (4, 0, 5, 2, 1, 3, 6, 7))
    return bias.reshape(L, 3, H, NA_N_CB, NA_QROWS * NA_KW, NA_KROWS * NA_BAND)


def _na_kernel(q_ref, k_ref, v_ref, bias_ref, o_ref, *, rows):
    step = pl.program_id(2)
    n_blocks = rows // NA_QROWS
    nq = NA_QROWS * NA_KW
    nk = NA_KROWS * NA_BAND
    lane = lax.broadcasted_iota(jnp.int32, (nq, LANES), 1)
    first_head = lane < NA_HEAD_DIM
    for sub in range(NA_BLOCKS_PER_STEP):
        i = step * NA_BLOCKS_PER_STEP + sub
        ks = pl.multiple_of(_na_key_row_start(i, rows) * NA_BAND, NA_BAND * (NA_KH // 2))
        variant = jnp.where(i == 0, 0, jnp.where(i == n_blocks - 1, 2, 1))
        for j in range(NA_N_CB):
            q = q_ref[0, j, sub * nq:(sub + 1) * nq, :]
            k = k_ref[0, j, pl.ds(ks, nk), :]
            v = v_ref[0, j, pl.ds(ks, nk), :]
            zero = jnp.zeros_like(q)
            q2 = jnp.concatenate([jnp.where(first_head, q, zero), jnp.where(first_head, zero, q)], axis=0)
            s = _dot_nt(q2, k) + bias_ref[variant, :, j].reshape(NA_HEADS_PER_STEP * nq, nk)
            m = jnp.max(s, axis=-1, keepdims=True)
            p = jnp.exp(s - m)
            l = jnp.sum(p, axis=-1, keepdims=True)
            o2 = _dot(p.astype(BF16), v) / l
            o = jnp.where(first_head, o2[:nq], o2[nq:]).astype(BF16)
            for a in range(NA_QROWS):
                row0 = (sub * NA_QROWS + a) * GRID_W + j * NA_KW
                o_ref[0, row0:row0 + NA_KW, :] = o[a * NA_KW:(a + 1) * NA_KW, :]


def _na_call(layer, q_cb, k_band, v_band, bias):
    B, _, tq, _ = q_cb.shape
    T = tq * NA_N_CB
    rows = T // GRID_W
    n_steps = rows // (NA_QROWS * NA_BLOCKS_PER_STEP)
    n_pairs = NA_HEADS // NA_HEADS_PER_STEP
    nq = NA_QROWS * NA_KW
    nk = NA_KROWS * NA_BAND
    kv_spec = pl.BlockSpec((1, NA_N_CB, T // 2, LANES), lambda hp, b, i: (b, 0, 0, hp))
    return pl.pallas_call(
        functools.partial(_na_kernel, rows=rows),
        grid=(n_pairs, B, n_steps),
        in_specs=[
            pl.BlockSpec((1, NA_N_CB, nq * NA_BLOCKS_PER_STEP, LANES), lambda hp, b, i: (b, 0, i, hp)),
            kv_spec, kv_spec,
            pl.BlockSpec((None, 3, NA_HEADS_PER_STEP, NA_N_CB, nq, nk),
                         lambda hp, b, i: (layer, 0, hp, 0, 0, 0), pipeline_mode=pl.Buffered(1)),
        ],
        out_specs=pl.BlockSpec((1, NA_QROWS * NA_BLOCKS_PER_STEP * GRID_W, LANES), lambda hp, b, i: (b, i, hp)),
        out_shape=jax.ShapeDtypeStruct((B, T, NA_W), BF16),
        compiler_params=pltpu.CompilerParams(
            dimension_semantics=("arbitrary", "arbitrary", "arbitrary"), vmem_limit_bytes=VMEM_LIMIT_BYTES),
        name="na_attn",
    )(q_cb, k_band, v_band, bias)


def _lru_gate_weights(wa, wi):
    L = wa.shape[0]
    ng = LRU_BLOCKS // LRU_GROUP

    def blockdiag(w):
        w = w.reshape(L, 2, ng, LRU_GROUP, LRU_BW, LRU_BW)
        eye = jnp.eye(LRU_GROUP, dtype=w.dtype)
        full = w[:, :, :, :, :, None, :] * eye[:, None, :, None]
        return full.reshape(L, 2, ng, LANES, LANES)

    return jnp.concatenate([blockdiag(wa), blockdiag(wi)], axis=-1).astype(BF16)


def _lru_gate_bias(ba, bi):
    L = ba.shape[0]
    ng = LRU_W // LANES
    return jnp.concatenate([ba.reshape(L, 2, ng, 1, LANES), bi.reshape(L, 2, ng, 1, LANES)], axis=-1)


def _softplus(x):
    return jnp.maximum(x, 0.0) + jnp.log1p(jnp.exp(-jnp.abs(x)))


def _lru_inputs(prev_ref, x_ref, next_ref, has_prev, has_next, cw_ref, cb_ref, w_ref, b_ref, lam_ref, d,
                a_scr, u_scr):
    ct = x_ref.shape[1]
    prev = jnp.where(has_prev, prev_ref[0], 0.0)
    nxt = jnp.where(has_next, next_ref[0], 0.0)
    xe = jnp.concatenate([prev, x_ref[0], nxt], axis=0)
    left = CONV_W // 2
    xc = cb_ref[...]
    n_ext = ct + 2 * SUBLANES
    for t in range(CONV_W):
        shift = (left - t) % n_ext
        xs = xe if shift == 0 else pltpu.roll(xe, shift, 0)
        xc = xc + xs[SUBLANES:SUBLANES + ct, :] * cw_ref[t:t + 1, :]
    xcb = xc.astype(BF16)
    sp = _softplus(-lam_ref[d:d + 1, :])
    for g in range(LRU_W // LANES):
        sl = slice(g * LANES, (g + 1) * LANES)
        z = _dot(xcb[:, sl], w_ref[d, g]) + b_ref[d, g]
        r = jax.nn.sigmoid(z[:, :LANES])
        gi = jax.nn.sigmoid(z[:, LANES:])
        log_a = (-LRU_C) * r * sp[:, sl]
        a = jnp.exp(log_a)
        y = -jnp.tanh(log_a) * (a * a + 1.0)
        u = (y * lax.rsqrt(jnp.maximum(y, F32_TINY))) * (gi * xc[:, sl])
        a_scr[:, sl] = a
        u_scr[:, sl] = u


def _slab_scan(a, b, reverse):
    row = lax.broadcasted_iota(jnp.int32, a.shape, 0)
    for sh in (1, 2, 4):
        if reverse:
            ok = row < SUBLANES - sh
            a_sh = pltpu.roll(a, SUBLANES - sh, 0)
            b_sh = pltpu.roll(b, SUBLANES - sh, 0)
        else:
            ok = row >= sh
            a_sh = pltpu.roll(a, sh, 0)
            b_sh = pltpu.roll(b, sh, 0)
        b = a * jnp.where(ok, b_sh, 0.0) + b
        a = a * jnp.where(ok, a_sh, 1.0)
    return a, b


def _lru_kernel(xf_ref, pf_ref, nf_ref, xb_ref, pb_ref, nb_ref, cw_ref, cb_ref, w_ref, b_ref, lam_ref,
                hf_ref, hb_ref, af_scr, uf_scr, ab_scr, ub_scr, cf_scr, cbk_scr):
    c = pl.program_id(1)
    nc = pl.num_programs(1)
    ct = xf_ref.shape[1]
    n_slabs = ct // SUBLANES

    @pl.when(c == 0)
    def _():
        cf_scr[...] = jnp.zeros_like(cf_scr)
        cbk_scr[...] = jnp.zeros_like(cbk_scr)

    _lru_inputs(pf_ref, xf_ref, nf_ref, c > 0, c < nc - 1, cw_ref, cb_ref, w_ref, b_ref, lam_ref, 0,
                af_scr, uf_scr)
    _lru_inputs(pb_ref, xb_ref, nb_ref, c < nc - 1, c > 0, cw_ref, cb_ref, w_ref, b_ref, lam_ref, 1,
                ab_scr, ub_scr)

    def body(s, carry):
        cf, cb = carry
        of = pl.multiple_of(s * SUBLANES, SUBLANES)
        a, b = _slab_scan(af_scr[pl.ds(of, SUBLANES), :], uf_scr[pl.ds(of, SUBLANES), :], False)
        h = b + a * cf
        hf_ref[0, pl.ds(of, SUBLANES), :] = h
        cf = h[SUBLANES - 1:SUBLANES, :]
        ob = pl.multiple_of((n_slabs - 1 - s) * SUBLANES, SUBLANES)
        a, b = _slab_scan(ab_scr[pl.ds(ob, SUBLANES), :], ub_scr[pl.ds(ob, SUBLANES), :], True)
        h = b + a * cb
        hb_ref[0, pl.ds(ob, SUBLANES), :] = h
        cb = h[0:1, :]
        return cf, cb

    cf, cb = lax.fori_loop(0, n_slabs, body, (cf_scr[...], cbk_scr[...]), unroll=2)
    cf_scr[...] = cf
    cbk_scr[...] = cb


def _lru_call(layer, x_lru, conv_w, conv_b, w_gate, b_gate, lam):
    B, T, W = x_lru.shape
    ct = LRU_CHUNK
    nc = T // ct
    hb = ct // SUBLANES
    last_halo = T // SUBLANES - 1

    main_f = pl.BlockSpec((1, ct, W), lambda b, c: (b, c, 0))
    prev_f = pl.BlockSpec((1, SUBLANES, W), lambda b, c: (b, jnp.maximum(c * hb - 1, 0), 0))
    next_f = pl.BlockSpec((1, SUBLANES, W), lambda b, c: (b, jnp.minimum((c + 1) * hb, last_halo), 0))
    main_b = pl.BlockSpec((1, ct, W), lambda b, c: (b, nc - 1 - c, 0))
    prev_b = pl.BlockSpec((1, SUBLANES, W), lambda b, c: (b, jnp.maximum((nc - 1 - c) * hb - 1, 0), 0))
    next_b = pl.BlockSpec((1, SUBLANES, W), lambda b, c: (b, jnp.minimum((nc - c) * hb, last_halo), 0))
    params = (conv_w, conv_b, w_gate, b_gate, lam)
    return pl.pallas_call(
        _lru_kernel,
        grid=(B, nc),
        in_specs=[main_f, prev_f, next_f, main_b, prev_b, next_b] + [_resident(p.shape[1:], layer) for p in params],
        out_specs=(main_f, main_b),
        out_shape=(jax.ShapeDtypeStruct((B, T, W), F32), jax.ShapeDtypeStruct((B, T, W), F32)),
        scratch_shapes=[pltpu.VMEM((ct, W), F32)] * 4 + [pltpu.VMEM((1, W), F32)] * 2,
        compiler_params=pltpu.CompilerParams(dimension_semantics=("arbitrary", "arbitrary")),
        name="rg_lru",
    )(x_lru, x_lru, x_lru, x_lru, x_lru, x_lru, *params)


def _merge_ffn_kernel(x_ref, yna_ref, hf_ref, hb_ref, glru_ref, qca_ref, kv_ref,
                      gm_ref, wg_ref, bg_ref, wna_ref, wlru_ref, wca_ref, wout_ref, gpost_ref,
                      g1_ref, wup_ref, wdn_ref, g2_ref, o_ref):
    x1 = x_ref[0]
    d = x1.shape[-1]
    h = _rms(x1, gm_ref[...]).astype(BF16)

    y_lru = ((hf_ref[0] + hb_ref[0]) * jax.nn.gelu(glru_ref[0])).astype(BF16)

    q = qca_ref[0]
    heads = []
    for hd in range(CA_HEADS):
        sl = slice(hd * CA_HEAD_DIM, (hd + 1) * CA_HEAD_DIM)
        k = kv_ref[0, 0, :, hd * CA_HEAD_DIM:(hd + 1) * CA_HEAD_DIM]
        v = kv_ref[0, 0, :, CA_W + hd * CA_HEAD_DIM:CA_W + (hd + 1) * CA_HEAD_DIM]
        s = _dot_nt(q[:, sl], k) * (CA_HEAD_DIM ** -0.5)
        m = jnp.max(s, axis=-1, keepdims=True)
        p = jnp.exp(s - m)
        l = jnp.sum(p, axis=-1, keepdims=True)
        heads.append((_dot(p.astype(BF16), v) / l).astype(BF16))
    y_ca = jnp.concatenate(heads, axis=-1)

    def gate(n):
        return jax.nn.sigmoid(_dot(h, wg_ref[:, n * d:(n + 1) * d]) + bg_ref[:, n * d:(n + 1) * d])

    merged = gate(0) * _dot(yna_ref[0], wna_ref[...])
    merged = merged + gate(1) * _dot(y_lru, wlru_ref[...])
    merged = merged + gate(2) * _dot(y_ca, wca_ref[...])
    x2 = x1 + _rms(_dot(merged.astype(BF16), wout_ref[...]), gpost_ref[...])
    o_ref[0] = _swiglu_half_step(x2, g1_ref[...], wup_ref, wdn_ref, g2_ref[...])


def _merge_ffn_call(layer, x1, y_na, h_f, h_b, g_lru, q_ca, kv, params):
    B, T, D = x1.shape
    M = kv.shape[2]
    tm = TOKEN_TILE
    tok = lambda w: pl.BlockSpec((1, tm, w), lambda b, i: (b, i, 0))
    in_specs = [tok(D), tok(NA_W), tok(LRU_W), tok(LRU_W), tok(LRU_W), tok(CA_W),
                pl.BlockSpec((1, 1, M, 2 * CA_W), lambda b, i: (layer, b, 0, 0))]
    in_specs += [_resident(p.shape[1:], layer) for p in params]
    return pl.pallas_call(
        _merge_ffn_kernel,
        grid=(B, T // tm),
        in_specs=in_specs,
        out_specs=tok(D),
        out_shape=jax.ShapeDtypeStruct((B, T, D), F32),
        compiler_params=pltpu.CompilerParams(
            dimension_semantics=("arbitrary", "arbitrary"), vmem_limit_bytes=VMEM_LIMIT_BYTES),
        name="merge_ffn",
    )(x1, y_na, h_f, h_b, g_lru, q_ca, kv, *params)


def kernel(x_prompt, x_sample, mem_prompt, mem_sample, g_ffn1_pre, w_ffn1_up, w_ffn1_down, g_ffn1_post, g_mix_pre, w_in, na_rpb, conv_w, conv_b, lru_wa, lru_ba, lru_wi, lru_bi, lru_lambda, g_mem, w_mem_kv, w_gate, b_gate, w_branch_na, w_branch_lru, w_branch_ca, w_out, g_mix_post, g_ffn2_pre, w_ffn2_up, w_ffn2_down, g_ffn2_post):
    n_prompt = x_prompt.shape[0]
    assert x_prompt.shape[1:] == x_sample.shape[1:] and mem_prompt.shape[1:] == mem_sample.shape[1:]
    x = jnp.concatenate([x_prompt, x_sample], axis=0)
    mem = jnp.concatenate([mem_prompt, mem_sample], axis=0)
    B, T, D = x.shape
    L = w_in.shape[0]
    rows = T // GRID_W
    assert rows % (NA_QROWS * NA_BLOCKS_PER_STEP) == 0 and rows >= NA_KROWS
    assert T % LRU_CHUNK == 0 and T % TOKEN_TILE == 0
    assert w_ffn1_down.shape[1] % MXU_DIM == 0 and w_ffn2_down.shape[1] % MXU_DIM == 0

    bf = lambda w: w.astype(BF16)
    vec = lambda g: g.reshape(L, 1, g.shape[-1])
    w1u, w1d, w2u, w2d = bf(w_ffn1_up), bf(w_ffn1_down), bf(w_ffn2_up), bf(w_ffn2_down)
    win, wg, wout, wkv = bf(w_in), bf(w_gate), bf(w_out), bf(w_mem_kv)
    wna, wlru, wca = bf(w_branch_na), bf(w_branch_lru), bf(w_branch_ca)
    lru_w = _lru_gate_weights(lru_wa, lru_wi)
    lru_b = _lru_gate_bias(lru_ba, lru_bi)
    bias = _na_bias_tables(na_rpb, rows)
    kv = _mem_kv_call(mem, vec(g_mem), wkv)

    for l in range(L):
        x1, q_cb, k_band, v_band, x_lru, g_lru, q_ca = _ffn_proj_call(
            l, x, vec(g_ffn1_pre), w1u, w1d, vec(g_ffn1_post), vec(g_mix_pre), win)
        y_na = _na_call(l, q_cb, k_band, v_band, bias)
        h_f, h_b = _lru_call(l, x_lru, conv_w, vec(conv_b), lru_w, lru_b, lru_lambda)
        x = _merge_ffn_call(
            l, x1, y_na, h_f, h_b, g_lru, q_ca, kv,
            (vec(g_mix_pre), wg, vec(b_gate), wna, wlru, wca, wout, vec(g_mix_post),
             vec(g_ffn2_pre), w2u, w2d, vec(g_ffn2_post)))
    return x[:n_prompt], x[n_prompt:]
```

```python
import functools

import numpy as np
import jax
import jax.numpy as jnp
from jax import lax
from jax.experimental import pallas as pl
from jax.experimental.pallas import tpu as pltpu

GRID_W = 64
NA_HEADS = 8
NA_HEAD_DIM = 64
NA_W = NA_HEADS * NA_HEAD_DIM
NA_KH = 8
NA_KW = 16
NA_N_CB = GRID_W // NA_KW
NA_BAND = 2 * NA_KW
LRU_W = 512
LRU_BLOCKS = 8
LRU_BW = LRU_W // LRU_BLOCKS
CONV_W = 4
LRU_C = 8.0
CA_HEADS = 4
CA_HEAD_DIM = 128
CA_W = CA_HEADS * CA_HEAD_DIM
EPS = 1e-6
NEG = -1e30
F32_TINY = float(np.finfo(np.float32).tiny)

LANES = 128
SUBLANES = 8
MXU_DIM = 256
VMEM_LIMIT_BYTES = 56 * 1024 * 1024

TOKEN_TILE = 512
SUBTILES = 2
FF_CHUNK_TILES = 6
NA_QROWS = 8
NA_KROWS = 2 * NA_KH
NA_HEADS_PER_STEP = LANES // NA_HEAD_DIM
NA_BLOCKS_PER_STEP = 4
LRU_CHUNK = 512
LRU_GROUP = LANES // LRU_BW

BF16 = jnp.bfloat16
F32 = jnp.float32


def _rms(x, g):
    return x * lax.rsqrt(jnp.mean(x * x, axis=-1, keepdims=True) + EPS) * g


def _dot(a, b):
    return jnp.dot(a, b, preferred_element_type=F32)


def _dot_nt(a, b):
    return lax.dot_general(a, b, (((1,), (1,)), ((), ())), preferred_element_type=F32)


def _ff_chunks(d_ff):
    tiles = d_ff // MXU_DIM
    widths = []
    while tiles > 0:
        n = min(FF_CHUNK_TILES, tiles)
        widths.append(n * MXU_DIM)
        tiles -= n
    return widths


def _swiglu_half_step(xs, g_pre, w_up_ref, w_down_ref, g_post):
    d_ff = w_down_ref.shape[0]
    hs = [_rms(x, g_pre).astype(BF16) for x in xs]
    accs = [None] * len(xs)
    lo = 0
    for cw in _ff_chunks(d_ff):
        for t, h in enumerate(hs):
            a = _dot(h, w_up_ref[:, lo:lo + cw])
            b = _dot(h, w_up_ref[:, d_ff + lo:d_ff + lo + cw])
            act = (a * jax.nn.sigmoid(a) * b).astype(BF16)
            part = _dot(act, w_down_ref[lo:lo + cw, :])
            accs[t] = part if accs[t] is None else accs[t] + part
        lo += cw
    return [x + 0.5 * _rms(acc, g_post) for x, acc in zip(xs, accs)]


def _subtiles(n_tokens):
    tm = n_tokens // SUBTILES
    return [slice(t * tm, (t + 1) * tm) for t in range(SUBTILES)]


def _band_start(j):
    return int(np.clip(j * NA_KW - NA_KW // 2, 0, GRID_W - NA_BAND))


def _ffn_proj_kernel(x_ref, g1_ref, wup_ref, wdn_ref, g2_ref, gm_ref, win_ref,
                     x1_ref, qcb_ref, kband_ref, vband_ref, xlru_ref, glru_ref, qca_ref):
    toks = _subtiles(x_ref.shape[1])
    x1s = _swiglu_half_step([x_ref[0, tok, :] for tok in toks], g1_ref[...], wup_ref, wdn_ref, g2_ref[...])
    for tok, x1 in zip(toks, x1s):
        x1_ref[0, tok, :] = x1
    hs = [_rms(x1, gm_ref[...]).astype(BF16) for x1 in x1s]
    for t, (tok, h) in enumerate(zip(toks, hs)):
        proj = _dot(h, win_ref[...])
        rows = proj.shape[0] // GRID_W
        q = (proj[:, 0:NA_W] * (NA_HEAD_DIM ** -0.5)).reshape(rows, GRID_W, NA_W)
        k = proj[:, NA_W:2 * NA_W].reshape(rows, GRID_W, NA_W)
        v = proj[:, 2 * NA_W:3 * NA_W].reshape(rows, GRID_W, NA_W)
        for j in range(NA_N_CB):
            bs = _band_start(j)
            qs = slice(t * rows * NA_KW, (t + 1) * rows * NA_KW)
            ks = slice(t * rows * NA_BAND, (t + 1) * rows * NA_BAND)
            qcb_ref[0, j, qs, :] = q[:, j * NA_KW:(j + 1) * NA_KW, :].reshape(rows * NA_KW, NA_W).astype(BF16)
            kband_ref[0, j, ks, :] = k[:, bs:bs + NA_BAND, :].reshape(rows * NA_BAND, NA_W).astype(BF16)
            vband_ref[0, j, ks, :] = v[:, bs:bs + NA_BAND, :].reshape(rows * NA_BAND, NA_W).astype(BF16)
        o = 3 * NA_W
        xlru_ref[0, tok, :] = proj[:, o:o + LRU_W]
        glru_ref[0, tok, :] = proj[:, o + LRU_W:o + 2 * LRU_W]
        qca_ref[0, tok, :] = proj[:, o + 2 * LRU_W:o + 2 * LRU_W + CA_W].astype(BF16)


def _resident(shape, layer):
    nd = len(shape)
    return pl.BlockSpec((None,) + tuple(shape), lambda *_: (layer,) + (0,) * nd,
                        pipeline_mode=pl.Buffered(1))


def _ffn_proj_call(layer, x, g1, wup, wdn, g2, gm, win):
    B, T, D = x.shape
    tm = TOKEN_TILE
    tok = lambda w: pl.BlockSpec((1, tm, w), lambda b, i: (b, i, 0))
    out_shape = (
        jax.ShapeDtypeStruct((B, T, D), F32),
        jax.ShapeDtypeStruct((B, NA_N_CB, T // NA_N_CB, NA_W), BF16),
        jax.ShapeDtypeStruct((B, NA_N_CB, T // 2, NA_W), BF16),
        jax.ShapeDtypeStruct((B, NA_N_CB, T // 2, NA_W), BF16),
        jax.ShapeDtypeStruct((B, T, LRU_W), F32),
        jax.ShapeDtypeStruct((B, T, LRU_W), F32),
        jax.ShapeDtypeStruct((B, T, CA_W), BF16),
    )
    out_specs = (
        tok(D),
        pl.BlockSpec((1, NA_N_CB, tm // NA_N_CB, NA_W), lambda b, i: (b, 0, i, 0)),
        pl.BlockSpec((1, NA_N_CB, tm // 2, NA_W), lambda b, i: (b, 0, i, 0)),
        pl.BlockSpec((1, NA_N_CB, tm // 2, NA_W), lambda b, i: (b, 0, i, 0)),
        tok(LRU_W), tok(LRU_W), tok(CA_W),
    )
    in_specs = [tok(D)] + [_resident(w.shape[1:], layer) for w in (g1, wup, wdn, g2, gm, win)]
    return pl.pallas_call(
        _ffn_proj_kernel,
        grid=(B, T // tm),
        in_specs=in_specs,
        out_specs=out_specs,
        out_shape=out_shape,
        compiler_params=pltpu.CompilerParams(
            dimension_semantics=("arbitrary", "arbitrary"), vmem_limit_bytes=VMEM_LIMIT_BYTES),
        name="ffn_proj",
    )(x, g1, wup, wdn, g2, gm, win)


def _mem_kv_kernel(mem_ref, g_ref, w_ref, kv_ref):
    h = _rms(mem_ref[0], g_ref[...]).astype(BF16)
    kv_ref[0, 0] = _dot(h, w_ref[...]).astype(BF16)


def _mem_kv_call(mem, g_mem, w_kv):
    B, M, D = mem.shape
    L = w_kv.shape[0]
    return pl.pallas_call(
        _mem_kv_kernel,
        grid=(L, B),
        in_specs=[
            pl.BlockSpec((1, M, D), lambda l, b: (b, 0, 0)),
            pl.BlockSpec((None, 1, D), lambda l, b: (l, 0, 0)),
            pl.BlockSpec((None, D, 2 * CA_W), lambda l, b: (l, 0, 0)),
        ],
        out_specs=pl.BlockSpec((1, 1, M, 2 * CA_W), lambda l, b: (l, b, 0, 0)),
        out_shape=jax.ShapeDtypeStruct((L, B, M, 2 * CA_W), BF16),
        compiler_params=pltpu.CompilerParams(dimension_semantics=("arbitrary", "arbitrary")),
        name="mem_kv",
    )(mem, g_mem, w_kv)


def _na_key_row_start(i, rows):
    return jnp.clip(i * NA_QROWS - NA_KH // 2, 0, rows - NA_KROWS)


def _na_bias_tables(rpb, rows):
    L, H = rpb.shape[:2]
    n_blocks = rows // NA_QROWS
    n_dr = 2 * NA_KH - 1
    j = np.arange(NA_N_CB)[:, None, None]
    qc = j * NA_KW + np.arange(NA_KW)[None, :, None]
    bs = np.clip(j * NA_KW - NA_KW // 2, 0, GRID_W - NA_BAND)
    kc = bs + np.arange(NA_BAND)[None, None, :]
    ws = np.clip(qc - NA_KW // 2, 0, GRID_W - NA_KW)
    col_ok = (kc >= ws) & (kc < ws + NA_KW)
    rp = jnp.pad(rpb, ((0, 0), (0, 0), (0, 0), (NA_BAND, NA_BAND)))
    cols = []
    for jj in range(NA_N_CB):
        for q in range(NA_KW):
            shift = NA_BAND + int(bs[jj, 0, 0]) - (jj * NA_KW + q) + (NA_KW - 1)
            cols.append(rp[..., shift:shift + NA_BAND])
    t1 = jnp.stack(cols).reshape(NA_N_CB, NA_KW, L, H, n_dr, NA_BAND)
    t1 = jnp.where(jnp.asarray(col_ok)[:, :, None, None, None, :], t1, NEG)
    t1 = jnp.pad(t1, ((0, 0),) * 4 + ((NA_KROWS, NA_KROWS), (0, 0)), constant_values=NEG)
    a = np.arange(NA_QROWS)[:, None]
    e = np.arange(NA_KROWS)[None, :]
    runs, row_ok = [], []
    for i in (0, min(1, n_blocks - 1), n_blocks - 1):
        ks = int(np.clip(i * NA_QROWS - NA_KH // 2, 0, rows - NA_KROWS))
        r = i * NA_QROWS + a
        rs = np.clip(r - NA_KH // 2, 0, rows - NA_KH)
        row_ok.append((ks + e >= rs) & (ks + e < rs + NA_KH))
        for aa in range(NA_QROWS):
            start = NA_KROWS + ks - (i * NA_QROWS + aa) + (NA_KH - 1)
            runs.append(t1[:, :, :, :, start:start + NA_KROWS, :])
    bias = jnp.stack(runs).reshape(3, NA_QROWS, NA_N_CB, NA_KW, L, H, NA_KROWS, NA_BAND)
    row_ok = np.stack(row_ok)
    bias = jnp.where(jnp.asarray(row_ok)[:, :, None, None, None, None, :, None], bias, NEG)
    bias = jnp.transpose(bias, (4, 0, 5, 2, 1, 3, 6, 7))
    return bias.reshape(L, 3, H, NA_N_CB, NA_QROWS * NA_KW, NA_KROWS * NA_BAND)


def _na_kernel(q_ref, k_ref, v_ref, bias_ref, o_ref, *, rows):
    step = pl.program_id(2)
    n_blocks = rows // NA_QROWS
    nq = NA_QROWS * NA_KW
    nk = NA_KROWS * NA_BAND
    lane = lax.broadcasted_iota(jnp.int32, (nq, LANES), 1)
    first_head = lane < NA_HEAD_DIM
    units = [(sub, j) for sub in range(NA_BLOCKS_PER_STEP) for j in range(NA_N_CB)]

    def scores(sub, j):
        i = step * NA_BLOCKS_PER_STEP + sub
        ks = pl.multiple_of(_na_key_row_start(i, rows) * NA_BAND, NA_BAND * (NA_KH // 2))
        variant = jnp.where(i == 0, 0, jnp.where(i == n_blocks - 1, 2, 1))
        q = q_ref[0, j, sub * nq:(sub + 1) * nq, :]
        k = k_ref[0, j, pl.ds(ks, nk), :]
        zero = jnp.zeros_like(q)
        q2 = jnp.concatenate([jnp.where(first_head, q, zero), jnp.where(first_head, zero, q)], axis=0)
        return _dot_nt(q2, k) + bias_ref[variant, :, j].reshape(NA_HEADS_PER_STEP * nq, nk), ks

    def softmax(s):
        m = jnp.max(s, axis=-1, keepdims=True)
        p = jnp.exp(s - m)
        return p.astype(BF16), jnp.sum(p, axis=-1, keepdims=True)

    def finish(sub, j, p, l, ks):
        v = v_ref[0, j, pl.ds(ks, nk), :]
        o2 = _dot(p, v) / l
        o = jnp.where(first_head, o2[:nq], o2[nq:]).astype(BF16)
        for a in range(NA_QROWS):
            row0 = (sub * NA_QROWS + a) * GRID_W + j * NA_KW
            o_ref[0, row0:row0 + NA_KW, :] = o[a * NA_KW:(a + 1) * NA_KW, :]

    n = len(units)
    sc, pr = {}, {}
    for t in range(n + 2):
        if t < n:
            sc[t] = scores(*units[t])
        if 0 <= t - 1 < n:
            s, ks = sc.pop(t - 1)
            pr[t - 1] = softmax(s) + (ks,)
        if 0 <= t - 2 < n:
            finish(*units[t - 2], *pr.pop(t - 2))


def _na_call(layer, q_cb, k_band, v_band, bias):
    B, _, tq, _ = q_cb.shape
    T = tq * NA_N_CB
    rows = T // GRID_W
    n_steps = rows // (NA_QROWS * NA_BLOCKS_PER_STEP)
    n_pairs = NA_HEADS // NA_HEADS_PER_STEP
    nq = NA_QROWS * NA_KW
    nk = NA_KROWS * NA_BAND
    kv_spec = pl.BlockSpec((1, NA_N_CB, T // 2, LANES), lambda hp, b, i: (b, 0, 0, hp))
    return pl.pallas_call(
        functools.partial(_na_kernel, rows=rows),
        grid=(n_pairs, B, n_steps),
        in_specs=[
            pl.BlockSpec((1, NA_N_CB, nq * NA_BLOCKS_PER_STEP, LANES), lambda hp, b, i: (b, 0, i, hp)),
            kv_spec, kv_spec,
            pl.BlockSpec((None, 3, NA_HEADS_PER_STEP, NA_N_CB, nq, nk),
                         lambda hp, b, i: (layer, 0, hp, 0, 0, 0), pipeline_mode=pl.Buffered(1)),
        ],
        out_specs=pl.BlockSpec((1, NA_QROWS * NA_BLOCKS_PER_STEP * GRID_W, LANES), lambda hp, b, i: (b, i, hp)),
        out_shape=jax.ShapeDtypeStruct((B, T, NA_W), BF16),
        compiler_params=pltpu.CompilerParams(
            dimension_semantics=("arbitrary", "arbitrary", "arbitrary"), vmem_limit_bytes=VMEM_LIMIT_BYTES),
        name="na_attn",
    )(q_cb, k_band, v_band, bias)


def _lru_gate_weights(wa, wi):
    L = wa.shape[0]
    ng = LRU_BLOCKS // LRU_GROUP

    def blockdiag(w):
        w = w.reshape(L, 2, ng, LRU_GROUP, LRU_BW, LRU_BW)
        eye = jnp.eye(LRU_GROUP, dtype=w.dtype)
        full = w[:, :, :, :, :, None, :] * eye[:, None, :, None]
        return full.reshape(L, 2, ng, LANES, LANES)

    return jnp.concatenate([blockdiag(wa), blockdiag(wi)], axis=-1).astype(BF16)


def _lru_gate_bias(ba, bi):
    L = ba.shape[0]
    ng = LRU_W // LANES
    return jnp.concatenate([ba.reshape(L, 2, ng, 1, LANES), bi.reshape(L, 2, ng, 1, LANES)], axis=-1)


def _softplus(x):
    return jnp.maximum(x, 0.0) + jnp.log1p(jnp.exp(-jnp.abs(x)))


def _lru_inputs(prev_ref, x_ref, next_ref, has_prev, has_next, cw_ref, cb_ref, w_ref, b_ref, lam_ref, d,
                a_scr, u_scr):
    ct = x_ref.shape[1]
    prev = jnp.where(has_prev, prev_ref[0], 0.0)
    nxt = jnp.where(has_next, next_ref[0], 0.0)
    xe = jnp.concatenate([prev, x_ref[0], nxt], axis=0)
    left = CONV_W // 2
    xc = cb_ref[...]
    n_ext = ct + 2 * SUBLANES
    for t in range(CONV_W):
        shift = (left - t) % n_ext
        xs = xe if shift == 0 else pltpu.roll(xe, shift, 0)
        xc = xc + xs[SUBLANES:SUBLANES + ct, :] * cw_ref[t:t + 1, :]
    xcb = xc.astype(BF16)
    sp = _softplus(-lam_ref[d:d + 1, :])
    for g in range(LRU_W // LANES):
        sl = slice(g * LANES, (g + 1) * LANES)
        z = _dot(xcb[:, sl], w_ref[d, g]) + b_ref[d, g]
        r = jax.nn.sigmoid(z[:, :LANES])
        gi = jax.nn.sigmoid(z[:, LANES:])
        log_a = (-LRU_C) * r * sp[:, sl]
        a = jnp.exp(log_a)
        y = -jnp.tanh(log_a) * (a * a + 1.0)
        u = (y * lax.rsqrt(jnp.maximum(y, F32_TINY))) * (gi * xc[:, sl])
        a_scr[:, sl] = a
        u_scr[:, sl] = u


def _slab_scan(a, b, reverse):
    row = lax.broadcasted_iota(jnp.int32, a.shape, 0)
    for sh in (1, 2, 4):
        if reverse:
            ok = row < SUBLANES - sh
            a_sh = pltpu.roll(a, SUBLANES - sh, 0)
            b_sh = pltpu.roll(b, SUBLANES - sh, 0)
        else:
            ok = row >= sh
            a_sh = pltpu.roll(a, sh, 0)
            b_sh = pltpu.roll(b, sh, 0)
        b = a * jnp.where(ok, b_sh, 0.0) + b
        a = a * jnp.where(ok, a_sh, 1.0)
    return a, b


def _lru_kernel(xf_ref, pf_ref, nf_ref, xb_ref, pb_ref, nb_ref, cw_ref, cb_ref, w_ref, b_ref, lam_ref,
                hf_ref, hb_ref, af_scr, uf_scr, ab_scr, ub_scr, cf_scr, cbk_scr):
    c = pl.program_id(1)
    nc = pl.num_programs(1)
    ct = xf_ref.shape[1]
    n_slabs = ct // SUBLANES

    @pl.when(c == 0)
    def _():
        cf_scr[...] = jnp.zeros_like(cf_scr)
        cbk_scr[...] = jnp.zeros_like(cbk_scr)

    _lru_inputs(pf_ref, xf_ref, nf_ref, c > 0, c < nc - 1, cw_ref, cb_ref, w_ref, b_ref, lam_ref, 0,
                af_scr, uf_scr)
    _lru_inputs(pb_ref, xb_ref, nb_ref, c < nc - 1, c > 0, cw_ref, cb_ref, w_ref, b_ref, lam_ref, 1,
                ab_scr, ub_scr)

    def body(s, carry):
        cf, cb = carry
        of = pl.multiple_of(s * SUBLANES, SUBLANES)
        a, b = _slab_scan(af_scr[pl.ds(of, SUBLANES), :], uf_scr[pl.ds(of, SUBLANES), :], False)
        h = b + a * cf
        hf_ref[0, pl.ds(of, SUBLANES), :] = h
        cf = h[SUBLANES - 1:SUBLANES, :]
        ob = pl.multiple_of((n_slabs - 1 - s) * SUBLANES, SUBLANES)
        a, b = _slab_scan(ab_scr[pl.ds(ob, SUBLANES), :], ub_scr[pl.ds(ob, SUBLANES), :], True)
        h = b + a * cb
        hb_ref[0, pl.ds(ob, SUBLANES), :] = h
        cb = h[0:1, :]
        return cf, cb

    cf, cb = lax.fori_loop(0, n_slabs, body, (cf_scr[...], cbk_scr[...]), unroll=2)
    cf_scr[...] = cf
    cbk_scr[...] = cb


def _lru_call(layer, x_lru, conv_w, conv_b, w_gate, b_gate, lam):
    B, T, W = x_lru.shape
    ct = LRU_CHUNK
    nc = T // ct
    hb = ct // SUBLANES
    last_halo = T // SUBLANES - 1

    main_f = pl.BlockSpec((1, ct, W), lambda b, c: (b, c, 0))
    prev_f = pl.BlockSpec((1, SUBLANES, W), lambda b, c: (b, jnp.maximum(c * hb - 1, 0), 0))
    next_f = pl.BlockSpec((1, SUBLANES, W), lambda b, c: (b, jnp.minimum((c + 1) * hb, last_halo), 0))
    main_b = pl.BlockSpec((1, ct, W), lambda b, c: (b, nc - 1 - c, 0))
    prev_b = pl.BlockSpec((1, SUBLANES, W), lambda b, c: (b, jnp.maximum((nc - 1 - c) * hb - 1, 0), 0))
    next_b = pl.BlockSpec((1, SUBLANES, W), lambda b, c: (b, jnp.minimum((nc - c) * hb, last_halo), 0))
    params = (conv_w, conv_b, w_gate, b_gate, lam)
    return pl.pallas_call(
        _lru_kernel,
        grid=(B, nc),
        in_specs=[main_f, prev_f, next_f, main_b, prev_b, next_b] + [_resident(p.shape[1:], layer) for p in params],
        out_specs=(main_f, main_b),
        out_shape=(jax.ShapeDtypeStruct((B, T, W), F32), jax.ShapeDtypeStruct((B, T, W), F32)),
        scratch_shapes=[pltpu.VMEM((ct, W), F32)] * 4 + [pltpu.VMEM((1, W), F32)] * 2,
        compiler_params=pltpu.CompilerParams(dimension_semantics=("arbitrary", "arbitrary")),
        name="rg_lru",
    )(x_lru, x_lru, x_lru, x_lru, x_lru, x_lru, *params)


def _merge_ffn_kernel(x_ref, yna_ref, hf_ref, hb_ref, glru_ref, qca_ref, kv_ref,
                      gm_ref, wg_ref, bg_ref, wna_ref, wlru_ref, wca_ref, wout_ref, gpost_ref,
                      g1_ref, wup_ref, wdn_ref, g2_ref, o_ref):
    toks = _subtiles(x_ref.shape[1])
    d = x_ref.shape[-1]
    x1s = [x_ref[0, tok, :] for tok in toks]
    hs = [_rms(x1, gm_ref[...]).astype(BF16) for x1 in x1s]
    y_lrus = [((hf_ref[0, tok, :] + hb_ref[0, tok, :]) * jax.nn.gelu(glru_ref[0, tok, :])).astype(BF16)
              for tok in toks]

    heads = [[] for _ in toks]
    for hd in range(CA_HEADS):
        sl = slice(hd * CA_HEAD_DIM, (hd + 1) * CA_HEAD_DIM)
        k = kv_ref[0, 0, :, hd * CA_HEAD_DIM:(hd + 1) * CA_HEAD_DIM]
        v = kv_ref[0, 0, :, CA_W + hd * CA_HEAD_DIM:CA_W + (hd + 1) * CA_HEAD_DIM]
        for t, tok in enumerate(toks):
            s = _dot_nt(qca_ref[0, tok, sl], k) * (CA_HEAD_DIM ** -0.5)
            m = jnp.max(s, axis=-1, keepdims=True)
            p = jnp.exp(s - m)
            l = jnp.sum(p, axis=-1, keepdims=True)
            heads[t].append((_dot(p.astype(BF16), v) / l).astype(BF16))
    y_cas = [jnp.concatenate(hh, axis=-1) for hh in heads]

    def gate(h, n):
        return jax.nn.sigmoid(_dot(h, wg_ref[:, n * d:(n + 1) * d]) + bg_ref[:, n * d:(n + 1) * d])

    branches = ((lambda t: yna_ref[0, toks[t], :], wna_ref), (lambda t: y_lrus[t], wlru_ref),
                (lambda t: y_cas[t], wca_ref))
    merged = [None] * len(toks)
    for n, (y, w_ref) in enumerate(branches):
        for t, h in enumerate(hs):
            term = gate(h, n) * _dot(y(t), w_ref[...])
            merged[t] = term if merged[t] is None else merged[t] + term
    x2s = [x1 + _rms(_dot(mg.astype(BF16), wout_ref[...]), gpost_ref[...]) for x1, mg in zip(x1s, merged)]
    outs = _swiglu_half_step(x2s, g1_ref[...], wup_ref, wdn_ref, g2_ref[...])
    for tok, out in zip(toks, outs):
        o_ref[0, tok, :] = out


def _merge_ffn_call(layer, x1, y_na, h_f, h_b, g_lru, q_ca, kv, params):
    B, T, D = x1.shape
    M = kv.shape[2]
    tm = TOKEN_TILE
    tok = lambda w: pl.BlockSpec((1, tm, w), lambda b, i: (b, i, 0))
    in_specs = [tok(D), tok(NA_W), tok(LRU_W), tok(LRU_W), tok(LRU_W), tok(CA_W),
                pl.BlockSpec((1, 1, M, 2 * CA_W), lambda b, i: (layer, b, 0, 0))]
    in_specs += [_resident(p.shape[1:], layer) for p in params]
    return pl.pallas_call(
        _merge_ffn_kernel,
        grid=(B, T // tm),
        in_specs=in_specs,
        out_specs=tok(D),
        out_shape=jax.ShapeDtypeStruct((B, T, D), F32),
        compiler_params=pltpu.CompilerParams(
            dimension_semantics=("arbitrary", "arbitrary"), vmem_limit_bytes=VMEM_LIMIT_BYTES),
        name="merge_ffn",
    )(x1, y_na, h_f, h_b, g_lru, q_ca, kv, *params)


def kernel(x_prompt, x_sample, mem_prompt, mem_sample, g_ffn1_pre, w_ffn1_up, w_ffn1_down, g_ffn1_post, g_mix_pre, w_in, na_rpb, conv_w, conv_b, lru_wa, lru_ba, lru_wi, lru_bi, lru_lambda, g_mem, w_mem_kv, w_gate, b_gate, w_branch_na, w_branch_lru, w_branch_ca, w_out, g_mix_post, g_ffn2_pre, w_ffn2_up, w_ffn2_down, g_ffn2_post):
    n_prompt = x_prompt.shape[0]
    assert x_prompt.shape[1:] == x_sample.shape[1:] and mem_prompt.shape[1:] == mem_sample.shape[1:]
    x = jnp.concatenate([x_prompt, x_sample], axis=0)
    mem = jnp.concatenate([mem_prompt, mem_sample], axis=0)
    B, T, D = x.shape
    L = w_in.shape[0]
    rows = T // GRID_W
    assert rows % (NA_QROWS * NA_BLOCKS_PER_STEP) == 0 and rows >= NA_KROWS
    assert T % LRU_CHUNK == 0 and T % TOKEN_TILE == 0
    assert w_ffn1_down.shape[1] % MXU_DIM == 0 and w_ffn2_down.shape[1] % MXU_DIM == 0

    bf = lambda w: w.astype(BF16)
    vec = lambda g: g.reshape(L, 1, g.shape[-1])
    w1u, w1d, w2u, w2d = bf(w_ffn1_up), bf(w_ffn1_down), bf(w_ffn2_up), bf(w_ffn2_down)
    win, wg, wout, wkv = bf(w_in), bf(w_gate), bf(w_out), bf(w_mem_kv)
    wna, wlru, wca = bf(w_branch_na), bf(w_branch_lru), bf(w_branch_ca)
    lru_w = _lru_gate_weights(lru_wa, lru_wi)
    lru_b = _lru_gate_bias(lru_ba, lru_bi)
    bias = _na_bias_tables(na_rpb, rows)
    kv = _mem_kv_call(mem, vec(g_mem), wkv)

    for l in range(L):
        x1, q_cb, k_band, v_band, x_lru, g_lru, q_ca = _ffn_proj_call(
            l, x, vec(g_ffn1_pre), w1u, w1d, vec(g_ffn1_post), vec(g_mix_pre), win)
        y_na = _na_call(l, q_cb, k_band, v_band, bias)
        h_f, h_b = _lru_call(l, x_lru, conv_w, vec(conv_b), lru_w, lru_b, lru_lambda)
        x = _merge_ffn_call(
            l, x1, y_na, h_f, h_b, g_lru, q_ca, kv,
            (vec(g_mix_pre), wg, vec(b_gate), wna, wlru, wca, wout, vec(g_mix_post),
             vec(g_ffn2_pre), w2u, w2d, vec(g_ffn2_post)))
    return x[:n_prompt], x[n_prompt:]
```

```python
import functools

import numpy as np
import jax
import jax.numpy as jnp
from jax import lax
from jax.experimental import pallas as pl
from jax.experimental.pallas import tpu as pltpu

GRID_W = 64
NA_HEADS = 8
NA_HEAD_DIM = 64
NA_W = NA_HEADS * NA_HEAD_DIM
NA_KH = 8
NA_KW = 16
NA_N_CB = GRID_W // NA_KW
NA_BAND = 2 * NA_KW
LRU_W = 512
LRU_BLOCKS = 8
LRU_BW = LRU_W // LRU_BLOCKS
CONV_W = 4
LRU_C = 8.0
CA_HEADS = 4
CA_HEAD_DIM = 128
CA_W = CA_HEADS * CA_HEAD_DIM
EPS = 1e-6
NEG = -1e30
F32_TINY = float(np.finfo(np.float32).tiny)

LANES = 128
SUBLANES = 8
MXU_DIM = 256
VMEM_LIMIT_BYTES = 56 * 1024 * 1024

TOKEN_TILE = 512
SUBTILES = 2
FF_CHUNK_TILES = 6
NA_QROWS = 8
NA_KROWS = 2 * NA_KH
NA_HEADS_PER_STEP = LANES // NA_HEAD_DIM
NA_BLOCKS_PER_STEP = 4
LRU_CHUNK = 512
LRU_GROUP = LANES // LRU_BW

BF16 = jnp.bfloat16
F32 = jnp.float32


def _rms(x, g):
    return x * lax.rsqrt(jnp.mean(x * x, axis=-1, keepdims=True) + EPS) * g


def _dot(a, b):
    return jnp.dot(a, b, preferred_element_type=F32)


def _dot_nt(a, b):
    return lax.dot_general(a, b, (((1,), (1,)), ((), ())), preferred_element_type=F32)


def _ff_chunks(d_ff):
    tiles = d_ff // MXU_DIM
    widths = []
    while tiles > 0:
        n = min(FF_CHUNK_TILES, tiles)
        widths.append(n * MXU_DIM)
        tiles -= n
    return widths


def _swiglu_half_step(xs, g_pre, w_up_ref, w_down_ref, g_post):
    d_ff = w_down_ref.shape[0]
    hs = [_rms(x, g_pre).astype(BF16) for x in xs]
    accs = [None] * len(xs)
    lo = 0
    for cw in _ff_chunks(d_ff):
        for t, h in enumerate(hs):
            a = _dot(h, w_up_ref[:, lo:lo + cw])
            b = _dot(h, w_up_ref[:, d_ff + lo:d_ff + lo + cw])
            act = (a * jax.nn.sigmoid(a) * b).astype(BF16)
            part = _dot(act, w_down_ref[lo:lo + cw, :])
            accs[t] = part if accs[t] is None else accs[t] + part
        lo += cw
    return [x + 0.5 * _rms(acc, g_post) for x, acc in zip(xs, accs)]


def _subtiles(n_tokens):
    tm = n_tokens // SUBTILES
    return [slice(t * tm, (t + 1) * tm) for t in range(SUBTILES)]


def _time_major(x):
    n, w = x.shape
    return jnp.swapaxes(x.reshape(SUBLANES, n // SUBLANES, w), 0, 1).reshape(n, w)


def _segment_major(x):
    n, w = x.shape
    return jnp.swapaxes(x.reshape(n // SUBLANES, SUBLANES, w), 0, 1).reshape(n, w)


def _band_start(j):
    return int(np.clip(j * NA_KW - NA_KW // 2, 0, GRID_W - NA_BAND))


def _ffn_proj_kernel(x_ref, g1_ref, wup_ref, wdn_ref, g2_ref, gm_ref, win_ref,
                     x1_ref, qcb_ref, kband_ref, vband_ref, xlru_ref, glru_ref, qca_ref):
    toks = _subtiles(x_ref.shape[1])
    x1s = _swiglu_half_step([x_ref[0, tok, :] for tok in toks], g1_ref[...], wup_ref, wdn_ref, g2_ref[...])
    for tok, x1 in zip(toks, x1s):
        x1_ref[0, tok, :] = x1
    hs = [_rms(x1, gm_ref[...]).astype(BF16) for x1 in x1s]
    o = 3 * NA_W
    lru = _time_major(jnp.concatenate([_dot(h, win_ref[:, o:o + 2 * LRU_W]) for h in hs], axis=0))
    xlru_ref[0] = lru[:, :LRU_W]
    glru_ref[0] = lru[:, LRU_W:]
    for t, (tok, h) in enumerate(zip(toks, hs)):
        qkv = _dot(h, win_ref[:, 0:o])
        rows = qkv.shape[0] // GRID_W
        q = (qkv[:, 0:NA_W] * (NA_HEAD_DIM ** -0.5)).reshape(rows, GRID_W, NA_W)
        k = qkv[:, NA_W:2 * NA_W].reshape(rows, GRID_W, NA_W)
        v = qkv[:, 2 * NA_W:3 * NA_W].reshape(rows, GRID_W, NA_W)
        for j in range(NA_N_CB):
            bs = _band_start(j)
            qs = slice(t * rows * NA_KW, (t + 1) * rows * NA_KW)
            ks = slice(t * rows * NA_BAND, (t + 1) * rows * NA_BAND)
            qcb_ref[0, j, qs, :] = q[:, j * NA_KW:(j + 1) * NA_KW, :].reshape(rows * NA_KW, NA_W).astype(BF16)
            kband_ref[0, j, ks, :] = k[:, bs:bs + NA_BAND, :].reshape(rows * NA_BAND, NA_W).astype(BF16)
            vband_ref[0, j, ks, :] = v[:, bs:bs + NA_BAND, :].reshape(rows * NA_BAND, NA_W).astype(BF16)
        qca_ref[0, tok, :] = _dot(h, win_ref[:, o + 2 * LRU_W:o + 2 * LRU_W + CA_W]).astype(BF16)


def _resident(shape, layer):
    nd = len(shape)
    return pl.BlockSpec((None,) + tuple(shape), lambda *_: (layer,) + (0,) * nd,
                        pipeline_mode=pl.Buffered(1))


def _ffn_proj_call(layer, x, g1, wup, wdn, g2, gm, win):
    B, T, D = x.shape
    tm = TOKEN_TILE
    tok = lambda w: pl.BlockSpec((1, tm, w), lambda b, i: (b, i, 0))
    out_shape = (
        jax.ShapeDtypeStruct((B, T, D), F32),
        jax.ShapeDtypeStruct((B, NA_N_CB, T // NA_N_CB, NA_W), BF16),
        jax.ShapeDtypeStruct((B, NA_N_CB, T // 2, NA_W), BF16),
        jax.ShapeDtypeStruct((B, NA_N_CB, T // 2, NA_W), BF16),
        jax.ShapeDtypeStruct((B, T, LRU_W), F32),
        jax.ShapeDtypeStruct((B, T, LRU_W), F32),
        jax.ShapeDtypeStruct((B, T, CA_W), BF16),
    )
    out_specs = (
        tok(D),
        pl.BlockSpec((1, NA_N_CB, tm // NA_N_CB, NA_W), lambda b, i: (b, 0, i, 0)),
        pl.BlockSpec((1, NA_N_CB, tm // 2, NA_W), lambda b, i: (b, 0, i, 0)),
        pl.BlockSpec((1, NA_N_CB, tm // 2, NA_W), lambda b, i: (b, 0, i, 0)),
        tok(LRU_W), tok(LRU_W), tok(CA_W),
    )
    in_specs = [tok(D)] + [_resident(w.shape[1:], layer) for w in (g1, wup, wdn, g2, gm, win)]
    return pl.pallas_call(
        _ffn_proj_kernel,
        grid=(B, T // tm),
        in_specs=in_specs,
        out_specs=out_specs,
        out_shape=out_shape,
        compiler_params=pltpu.CompilerParams(
            dimension_semantics=("arbitrary", "arbitrary"), vmem_limit_bytes=VMEM_LIMIT_BYTES),
        name="ffn_proj",
    )(x, g1, wup, wdn, g2, gm, win)


def _mem_kv_kernel(mem_ref, g_ref, w_ref, kv_ref):
    h = _rms(mem_ref[0], g_ref[...]).astype(BF16)
    kv_ref[0, 0] = _dot(h, w_ref[...]).astype(BF16)


def _mem_kv_call(mem, g_mem, w_kv):
    B, M, D = mem.shape
    L = w_kv.shape[0]
    return pl.pallas_call(
        _mem_kv_kernel,
        grid=(L, B),
        in_specs=[
            pl.BlockSpec((1, M, D), lambda l, b: (b, 0, 0)),
            pl.BlockSpec((None, 1, D), lambda l, b: (l, 0, 0)),
            pl.BlockSpec((None, D, 2 * CA_W), lambda l, b: (l, 0, 0)),
        ],
        out_specs=pl.BlockSpec((1, 1, M, 2 * CA_W), lambda l, b: (l, b, 0, 0)),
        out_shape=jax.ShapeDtypeStruct((L, B, M, 2 * CA_W), BF16),
        compiler_params=pltpu.CompilerParams(dimension_semantics=("arbitrary", "arbitrary")),
        name="mem_kv",
    )(mem, g_mem, w_kv)


def _na_key_row_start(i, rows):
    return jnp.clip(i * NA_QROWS - NA_KH // 2, 0, rows - NA_KROWS)


def _na_bias_tables(rpb, rows):
    L, H = rpb.shape[:2]
    n_blocks = rows // NA_QROWS
    n_dr = 2 * NA_KH - 1
    j = np.arange(NA_N_CB)[:, None, None]
    qc = j * NA_KW + np.arange(NA_KW)[None, :, None]
    bs = np.clip(j * NA_KW - NA_KW // 2, 0, GRID_W - NA_BAND)
    kc = bs + np.arange(NA_BAND)[None, None, :]
    ws = np.clip(qc - NA_KW // 2, 0, GRID_W - NA_KW)
    col_ok = (kc >= ws) & (kc < ws + NA_KW)
    rp = jnp.pad(rpb, ((0, 0), (0, 0), (0, 0), (NA_BAND, NA_BAND)))
    cols = []
    for jj in range(NA_N_CB):
        for q in range(NA_KW):
            shift = NA_BAND + int(bs[jj, 0, 0]) - (jj * NA_KW + q) + (NA_KW - 1)
            cols.append(rp[..., shift:shift + NA_BAND])
    t1 = jnp.stack(cols).reshape(NA_N_CB, NA_KW, L, H, n_dr, NA_BAND)
    t1 = jnp.where(jnp.asarray(col_ok)[:, :, None, None, None, :], t1, NEG)
    t1 = jnp.pad(t1, ((0, 0),) * 4 + ((NA_KROWS, NA_KROWS), (0, 0)), constant_values=NEG)
    a = np.arange(NA_QROWS)[:, None]
    e = np.arange(NA_KROWS)[None, :]
    runs, row_ok = [], []
    for i in (0, min(1, n_blocks - 1), n_blocks - 1):
        ks = int(np.clip(i * NA_QROWS - NA_KH // 2, 0, rows - NA_KROWS))
        r = i * NA_QROWS + a
        rs = np.clip(r - NA_KH // 2, 0, rows - NA_KH)
        row_ok.append((ks + e >= rs) & (ks + e < rs + NA_KH))
        for aa in range(NA_QROWS):
            start = NA_KROWS + ks - (i * NA_QROWS + aa) + (NA_KH - 1)
            runs.append(t1[:, :, :, :, start:start + NA_KROWS, :])
    bias = jnp.stack(runs).reshape(3, NA_QROWS, NA_N_CB, NA_KW, L, H, NA_KROWS, NA_BAND)
    row_ok = np.stack(row_ok)
    bias = jnp.where(jnp.asarray(row_ok)[:, :, None, None, None, None, :, None], bias, NEG)
    bias = jnp.transpose(bias, (4, 0, 5, 2, 1, 3, 6, 7))
    return bias.reshape(L, 3, H, NA_N_CB, NA_QROWS * NA_KW, NA_KROWS * NA_BAND)


def _na_kernel(q_ref, k_ref, v_ref, bias_ref, o_ref, *, rows):
    step = pl.program_id(2)
    n_blocks = rows // NA_QROWS
    nq = NA_QROWS * NA_KW
    nk = NA_KROWS * NA_BAND
    lane = lax.broadcasted_iota(jnp.int32, (nq, LANES), 1)
    first_head = lane < NA_HEAD_DIM
    units = [(sub, j) for sub in range(NA_BLOCKS_PER_STEP) for j in range(NA_N_CB)]

    def scores(sub, j):
        i = step * NA_BLOCKS_PER_STEP + sub
        ks = pl.multiple_of(_na_key_row_start(i, rows) * NA_BAND, NA_BAND * (NA_KH // 2))
        variant = jnp.where(i == 0, 0, jnp.where(i == n_blocks - 1, 2, 1))
        q = q_ref[0, j, sub * nq:(sub + 1) * nq, :]
        k = k_ref[0, j, pl.ds(ks, nk), :]
        zero = jnp.zeros_like(q)
        q2 = jnp.concatenate([jnp.where(first_head, q, zero), jnp.where(first_head, zero, q)], axis=0)
        return _dot_nt(q2, k) + bias_ref[variant, :, j].reshape(NA_HEADS_PER_STEP * nq, nk), ks

    def softmax(s):
        m = jnp.max(s, axis=-1, keepdims=True)
        p = jnp.exp(s - m)
        return p.astype(BF16), jnp.sum(p, axis=-1, keepdims=True)

    def finish(sub, j, p, l, ks):
        v = v_ref[0, j, pl.ds(ks, nk), :]
        o2 = _dot(p, v) / l
        o = jnp.where(first_head, o2[:nq], o2[nq:]).astype(BF16)
        for a in range(NA_QROWS):
            row0 = (sub * NA_QROWS + a) * GRID_W + j * NA_KW
            o_ref[0, row0:row0 + NA_KW, :] = o[a * NA_KW:(a + 1) * NA_KW, :]

    n = len(units)
    sc, pr = {}, {}
    for t in range(n + 2):
        if t < n:
            sc[t] = scores(*units[t])
        if 0 <= t - 1 < n:
            s, ks = sc.pop(t - 1)
            pr[t - 1] = softmax(s) + (ks,)
        if 0 <= t - 2 < n:
            finish(*units[t - 2], *pr.pop(t - 2))


def _na_call(layer, q_cb, k_band, v_band, bias):
    B, _, tq, _ = q_cb.shape
    T = tq * NA_N_CB
    rows = T // GRID_W
    n_steps = rows // (NA_QROWS * NA_BLOCKS_PER_STEP)
    n_pairs = NA_HEADS // NA_HEADS_PER_STEP
    nq = NA_QROWS * NA_KW
    nk = NA_KROWS * NA_BAND
    kv_spec = pl.BlockSpec((1, NA_N_CB, T // 2, LANES), lambda hp, b, i: (b, 0, 0, hp))
    return pl.pallas_call(
        functools.partial(_na_kernel, rows=rows),
        grid=(n_pairs, B, n_steps),
        in_specs=[
            pl.BlockSpec((1, NA_N_CB, nq * NA_BLOCKS_PER_STEP, LANES), lambda hp, b, i: (b, 0, i, hp)),
            kv_spec, kv_spec,
            pl.BlockSpec((None, 3, NA_HEADS_PER_STEP, NA_N_CB, nq, nk),
                         lambda hp, b, i: (layer, 0, hp, 0, 0, 0), pipeline_mode=pl.Buffered(1)),
        ],
        out_specs=pl.BlockSpec((1, NA_QROWS * NA_BLOCKS_PER_STEP * GRID_W, LANES), lambda hp, b, i: (b, i, hp)),
        out_shape=jax.ShapeDtypeStruct((B, T, NA_W), BF16),
        compiler_params=pltpu.CompilerParams(
            dimension_semantics=("arbitrary", "arbitrary", "arbitrary"), vmem_limit_bytes=VMEM_LIMIT_BYTES),
        name="na_attn",
    )(q_cb, k_band, v_band, bias)


def _lru_gate_weights(wa, wi):
    L = wa.shape[0]
    ng = LRU_BLOCKS // LRU_GROUP

    def blockdiag(w):
        w = w.reshape(L, 2, ng, LRU_GROUP, LRU_BW, LRU_BW)
        eye = jnp.eye(LRU_GROUP, dtype=w.dtype)
        full = w[:, :, :, :, :, None, :] * eye[:, None, :, None]
        return full.reshape(L, 2, ng, LANES, LANES)

    return jnp.concatenate([blockdiag(wa), blockdiag(wi)], axis=-1).astype(BF16)


def _lru_gate_bias(ba, bi):
    L = ba.shape[0]
    ng = LRU_W // LANES
    return jnp.concatenate([ba.reshape(L, 2, ng, 1, LANES), bi.reshape(L, 2, ng, 1, LANES)], axis=-1)


def _softplus(x):
    return jnp.maximum(x, 0.0) + jnp.log1p(jnp.exp(-jnp.abs(x)))


def _lru_inputs(prev_ref, x_ref, next_ref, has_prev, has_next, cw_ref, cb_ref, w_ref, b_ref, lam_ref, d,
                a_scr, u_scr):
    assert CONV_W == 4
    ct, w = x_ref.shape[1], x_ref.shape[2]
    nk = ct // SUBLANES
    x = x_ref[0].reshape(nk, SUBLANES, w)
    sub = lax.broadcasted_iota(jnp.int32, (SUBLANES, w), 0)
    prev = jnp.where(has_prev, prev_ref[0], 0.0)
    nxt = jnp.where(has_next, next_ref[0], 0.0)
    back1 = jnp.where(sub == 0, pltpu.roll(prev[SUBLANES:], 1, 0), pltpu.roll(x[nk - 1], 1, 0))
    back2 = jnp.where(sub == 0, pltpu.roll(prev[:SUBLANES], 1, 0), pltpu.roll(x[nk - 2], 1, 0))
    fwd1 = jnp.where(sub == SUBLANES - 1, pltpu.roll(nxt, SUBLANES - 1, 0), pltpu.roll(x[0], SUBLANES - 1, 0))
    taps = (
        jnp.concatenate([back2[None], back1[None], x[:nk - 2]], axis=0),
        jnp.concatenate([back1[None], x[:nk - 1]], axis=0),
        x,
        jnp.concatenate([x[1:], fwd1[None]], axis=0),
    )
    xc = cb_ref[...]
    for t in range(CONV_W):
        xc = xc + taps[t] * cw_ref[t:t + 1, :]
    xc = xc.reshape(ct, w)
    xcb = xc.astype(BF16)
    sp = _softplus(-lam_ref[d:d + 1, :])
    for g in range(LRU_W // LANES):
        sl = slice(g * LANES, (g + 1) * LANES)
        z = _dot(xcb[:, sl], w_ref[d, g]) + b_ref[d, g]
        r = jax.nn.sigmoid(z[:, :LANES])
        gi = jax.nn.sigmoid(z[:, LANES:])
        log_a = (-LRU_C) * r * sp[:, sl]
        a = jnp.exp(log_a)
        y = -jnp.tanh(log_a) * (a * a + 1.0)
        u = (y * lax.rsqrt(jnp.maximum(y, F32_TINY))) * (gi * xc[:, sl])
        a_scr[:, sl] = a
        u_scr[:, sl] = u


def _slab_scan(a, b, reverse):
    row = lax.broadcasted_iota(jnp.int32, a.shape, 0)
    for sh in (1, 2, 4):
        if reverse:
            ok = row < SUBLANES - sh
            a_sh = pltpu.roll(a, SUBLANES - sh, 0)
            b_sh = pltpu.roll(b, SUBLANES - sh, 0)
        else:
            ok = row >= sh
            a_sh = pltpu.roll(a, sh, 0)
            b_sh = pltpu.roll(b, sh, 0)
        b = a * jnp.where(ok, b_sh, 0.0) + b
        a = a * jnp.where(ok, a_sh, 1.0)
    return a, b


def _segment_scan(a_scr, u_scr, h_ref, carry_scr, reverse):
    ct, w = a_scr.shape
    nk = ct // SUBLANES

    def group(k):
        kk = (nk - 1 - k) if reverse else k
        return slice(kk * SUBLANES, (kk + 1) * SUBLANES)

    h = jnp.zeros((SUBLANES, w), F32)
    acc = jnp.ones((SUBLANES, w), F32)
    for k in range(nk):
        rows = group(k)
        a = a_scr[rows, :]
        h = a * h + u_scr[rows, :]
        acc = a * acc
        h_ref[0, rows, :] = h
        a_scr[rows, :] = acc
    a_inc, h_inc = _slab_scan(acc, h, reverse)
    state = h_inc + a_inc * carry_scr[...]
    sub = lax.broadcasted_iota(jnp.int32, (SUBLANES, w), 0)
    if reverse:
        carry_out = state[0:1, :]
        incoming = jnp.where(sub == SUBLANES - 1, carry_scr[...], pltpu.roll(state, SUBLANES - 1, 0))
    else:
        carry_out = state[SUBLANES - 1:SUBLANES, :]
        incoming = jnp.where(sub == 0, carry_scr[...], pltpu.roll(state, 1, 0))
    carry_scr[...] = carry_out
    for k in range(nk):
        rows = group(k)
        h_ref[0, rows, :] = h_ref[0, rows, :] + a_scr[rows, :] * incoming


def _lru_kernel(xf_ref, pf_ref, nf_ref, xb_ref, pb_ref, nb_ref, cw_ref, cb_ref, w_ref, b_ref, lam_ref,
                hf_ref, hb_ref, af_scr, uf_scr, ab_scr, ub_scr, cf_scr, cbk_scr):
    c = pl.program_id(1)
    nc = pl.num_programs(1)

    @pl.when(c == 0)
    def _():
        cf_scr[...] = jnp.zeros_like(cf_scr)
        cbk_scr[...] = jnp.zeros_like(cbk_scr)

    _lru_inputs(pf_ref, xf_ref, nf_ref, c > 0, c < nc - 1, cw_ref, cb_ref, w_ref, b_ref, lam_ref, 0,
                af_scr, uf_scr)
    _lru_inputs(pb_ref, xb_ref, nb_ref, c < nc - 1, c > 0, cw_ref, cb_ref, w_ref, b_ref, lam_ref, 1,
                ab_scr, ub_scr)
    _segment_scan(af_scr, uf_scr, hf_ref, cf_scr, False)
    _segment_scan(ab_scr, ub_scr, hb_ref, cbk_scr, True)


def _lru_call(layer, x_lru, conv_w, conv_b, w_gate, b_gate, lam):
    B, T, W = x_lru.shape
    ct = LRU_CHUNK
    nc = T // ct
    tail = 2 * SUBLANES
    n_tail = ct // tail
    n_head = ct // SUBLANES

    main_f = pl.BlockSpec((1, ct, W), lambda b, c: (b, c, 0))
    prev_f = pl.BlockSpec((1, tail, W), lambda b, c: (b, jnp.maximum(c * n_tail - 1, 0), 0))
    next_f = pl.BlockSpec((1, SUBLANES, W), lambda b, c: (b, jnp.minimum(c + 1, nc - 1) * n_head, 0))
    main_b = pl.BlockSpec((1, ct, W), lambda b, c: (b, nc - 1 - c, 0))
    prev_b = pl.BlockSpec((1, tail, W), lambda b, c: (b, jnp.maximum((nc - 1 - c) * n_tail - 1, 0), 0))
    next_b = pl.BlockSpec((1, SUBLANES, W), lambda b, c: (b, jnp.minimum(nc - c, nc - 1) * n_head, 0))
    params = (conv_w, conv_b, w_gate, b_gate, lam)
    return pl.pallas_call(
        _lru_kernel,
        grid=(B, nc),
        in_specs=[main_f, prev_f, next_f, main_b, prev_b, next_b] + [_resident(p.shape[1:], layer) for p in params],
        out_specs=(main_f, main_b),
        out_shape=(jax.ShapeDtypeStruct((B, T, W), F32), jax.ShapeDtypeStruct((B, T, W), F32)),
        scratch_shapes=[pltpu.VMEM((ct, W), F32)] * 4 + [pltpu.VMEM((1, W), F32)] * 2,
        compiler_params=pltpu.CompilerParams(dimension_semantics=("arbitrary", "arbitrary")),
        name="rg_lru",
    )(x_lru, x_lru, x_lru, x_lru, x_lru, x_lru, *params)


def _merge_ffn_kernel(x_ref, yna_ref, hf_ref, hb_ref, glru_ref, qca_ref, kv_ref,
                      gm_ref, wg_ref, bg_ref, wna_ref, wlru_ref, wca_ref, wout_ref, gpost_ref,
                      g1_ref, wup_ref, wdn_ref, g2_ref, o_ref):
    toks = _subtiles(x_ref.shape[1])
    d = x_ref.shape[-1]
    x1s = [x_ref[0, tok, :] for tok in toks]
    hs = [_rms(x1, gm_ref[...]).astype(BF16) for x1 in x1s]

    heads = [[] for _ in toks]
    for hd in range(CA_HEADS):
        sl = slice(hd * CA_HEAD_DIM, (hd + 1) * CA_HEAD_DIM)
        k = kv_ref[0, 0, :, hd * CA_HEAD_DIM:(hd + 1) * CA_HEAD_DIM]
        v = kv_ref[0, 0, :, CA_W + hd * CA_HEAD_DIM:CA_W + (hd + 1) * CA_HEAD_DIM]
        for t, tok in enumerate(toks):
            s = _dot_nt(qca_ref[0, tok, sl], k) * (CA_HEAD_DIM ** -0.5)
            m = jnp.max(s, axis=-1, keepdims=True)
            p = jnp.exp(s - m)
            l = jnp.sum(p, axis=-1, keepdims=True)
            heads[t].append((_dot(p.astype(BF16), v) / l).astype(BF16))
    y_cas = [jnp.concatenate(hh, axis=-1) for hh in heads]

    def gate(h, n):
        return jax.nn.sigmoid(_dot(h, wg_ref[:, n * d:(n + 1) * d]) + bg_ref[:, n * d:(n + 1) * d])

    merged = [gate(h, 0) * _dot(yna_ref[0, tok, :], wna_ref[...]) for tok, h in zip(toks, hs)]
    y_lru = _segment_major((hf_ref[0] + hb_ref[0]) * jax.nn.gelu(glru_ref[0]))
    merged = [mg + gate(h, 1) * _dot(y_lru[tok, :].astype(BF16), wlru_ref[...])
              for tok, h, mg in zip(toks, hs, merged)]
    merged = [mg + gate(h, 2) * _dot(y_ca, wca_ref[...]) for h, mg, y_ca in zip(hs, merged, y_cas)]
    x2s = [x1 + _rms(_dot(mg.astype(BF16), wout_ref[...]), gpost_ref[...]) for x1, mg in zip(x1s, merged)]
    outs = _swiglu_half_step(x2s, g1_ref[...], wup_ref, wdn_ref, g2_ref[...])
    for tok, out in zip(toks, outs):
        o_ref[0, tok, :] = out


def _merge_ffn_call(layer, x1, y_na, h_f, h_b, g_lru, q_ca, kv, params):
    B, T, D = x1.shape
    M = kv.shape[2]
    tm = TOKEN_TILE
    tok = lambda w: pl.BlockSpec((1, tm, w), lambda b, i: (b, i, 0))
    in_specs = [tok(D), tok(NA_W), tok(LRU_W), tok(LRU_W), tok(LRU_W), tok(CA_W),
                pl.BlockSpec((1, 1, M, 2 * CA_W), lambda b, i: (layer, b, 0, 0))]
    in_specs += [_resident(p.shape[1:], layer) for p in params]
    return pl.pallas_call(
        _merge_ffn_kernel,
        grid=(B, T // tm),
        in_specs=in_specs,
        out_specs=tok(D),
        out_shape=jax.ShapeDtypeStruct((B, T, D), F32),
        compiler_params=pltpu.CompilerParams(
            dimension_semantics=("arbitrary", "arbitrary"), vmem_limit_bytes=VMEM_LIMIT_BYTES),
        name="merge_ffn",
    )(x1, y_na, h_f, h_b, g_lru, q_ca, kv, *params)


def kernel(x_prompt, x_sample, mem_prompt, mem_sample, g_ffn1_pre, w_ffn1_up, w_ffn1_down, g_ffn1_post, g_mix_pre, w_in, na_rpb, conv_w, conv_b, lru_wa, lru_ba, lru_wi, lru_bi, lru_lambda, g_mem, w_mem_kv, w_gate, b_gate, w_branch_na, w_branch_lru, w_branch_ca, w_out, g_mix_post, g_ffn2_pre, w_ffn2_up, w_ffn2_down, g_ffn2_post):
    n_prompt = x_prompt.shape[0]
    assert x_prompt.shape[1:] == x_sample.shape[1:] and mem_prompt.shape[1:] == mem_sample.shape[1:]
    x = jnp.concatenate([x_prompt, x_sample], axis=0)
    mem = jnp.concatenate([mem_prompt, mem_sample], axis=0)
    B, T, D = x.shape
    L = w_in.shape[0]
    rows = T // GRID_W
    assert rows % (NA_QROWS * NA_BLOCKS_PER_STEP) == 0 and rows >= NA_KROWS
    assert T % TOKEN_TILE == 0 and LRU_CHUNK == TOKEN_TILE
    assert w_ffn1_down.shape[1] % MXU_DIM == 0 and w_ffn2_down.shape[1] % MXU_DIM == 0

    bf = lambda w: w.astype(BF16)
    vec = lambda g: g.reshape(L, 1, g.shape[-1])
    w1u, w1d, w2u, w2d = bf(w_ffn1_up), bf(w_ffn1_down), bf(w_ffn2_up), bf(w_ffn2_down)
    win, wg, wout, wkv = bf(w_in), bf(w_gate), bf(w_out), bf(w_mem_kv)
    wna, wlru, wca = bf(w_branch_na), bf(w_branch_lru), bf(w_branch_ca)
    lru_w = _lru_gate_weights(lru_wa, lru_wi)
    lru_b = _lru_gate_bias(lru_ba, lru_bi)
    bias = _na_bias_tables(na_rpb, rows)
    kv = _mem_kv_call(mem, vec(g_mem), wkv)

    for l in range(L):
        x1, q_cb, k_band, v_band, x_lru, g_lru, q_ca = _ffn_proj_call(
            l, x, vec(g_ffn1_pre), w1u, w1d, vec(g_ffn1_post), vec(g_mix_pre), win)
        y_na = _na_call(l, q_cb, k_band, v_band, bias)
        h_f, h_b = _lru_call(l, x_lru, conv_w, vec(conv_b), lru_w, lru_b, lru_lambda)
        x = _merge_ffn_call(
            l, x1, y_na, h_f, h_b, g_lru, q_ca, kv,
            (vec(g_mix_pre), wg, vec(b_gate), wna, wlru, wca, wout, vec(g_mix_post),
             vec(g_ffn2_pre), w2u, w2d, vec(g_ffn2_post)))
    return x[:n_prompt], x[n_prompt:]
```

```python
import functools

import numpy as np
import jax
import jax.numpy as jnp
from jax import lax
from jax.experimental import pallas as pl
from jax.experimental.pallas import tpu as pltpu

GRID_W = 64
NA_HEADS = 8
NA_HEAD_DIM = 64
NA_W = NA_HEADS * NA_HEAD_DIM
NA_KH = 8
NA_KW = 16
NA_N_CB = GRID_W // NA_KW
NA_BAND = 2 * NA_KW
LRU_W = 512
LRU_BLOCKS = 8
LRU_BW = LRU_W // LRU_BLOCKS
CONV_W = 4
LRU_C = 8.0
CA_HEADS = 4
CA_HEAD_DIM = 128
CA_W = CA_HEADS * CA_HEAD_DIM
EPS = 1e-6
NEG = -1e30
F32_TINY = float(np.finfo(np.float32).tiny)

LANES = 128
SUBLANES = 8
MXU_DIM = 256
VMEM_BYTES = 64 * 1024 * 1024
VMEM_LIMIT_BYTES = VMEM_BYTES - 4 * 1024 * 1024

TOKEN_TILE = 512
SUBTILES = 2
FF_CHUNK_TILES = 6
NA_QROWS = 8
NA_KROWS = 2 * NA_KH
NA_HEADS_PER_STEP = LANES // NA_HEAD_DIM
NA_BLOCKS_PER_STEP = 4
LRU_CHUNK = 512
LRU_GROUP = LANES // LRU_BW

BF16 = jnp.bfloat16
F32 = jnp.float32


def _rms(x, g):
    return x * lax.rsqrt(jnp.mean(x * x, axis=-1, keepdims=True) + EPS) * g


def _dot(a, b):
    return jnp.dot(a, b, preferred_element_type=F32)


def _dot_nt(a, b):
    return lax.dot_general(a, b, (((1,), (1,)), ((), ())), preferred_element_type=F32)


def _ff_chunks(d_ff):
    tiles = d_ff // MXU_DIM
    widths = []
    while tiles > 0:
        n = min(FF_CHUNK_TILES, tiles)
        widths.append(n * MXU_DIM)
        tiles -= n
    return widths


def _swiglu_half_step(xs, g_pre, w_up_ref, w_down_ref, g_post):
    d_ff = w_down_ref.shape[0]
    hs = [_rms(x, g_pre).astype(BF16) for x in xs]
    accs = [None] * len(xs)
    lo = 0
    for cw in _ff_chunks(d_ff):
        for t, h in enumerate(hs):
            a = _dot(h, w_up_ref[:, lo:lo + cw])
            b = _dot(h, w_up_ref[:, d_ff + lo:d_ff + lo + cw])
            act = (a * jax.nn.sigmoid(a) * b).astype(BF16)
            part = _dot(act, w_down_ref[lo:lo + cw, :])
            accs[t] = part if accs[t] is None else accs[t] + part
        lo += cw
    return [x + 0.5 * _rms(acc, g_post) for x, acc in zip(xs, accs)]


def _subtiles(n_tokens):
    tm = n_tokens // SUBTILES
    return [slice(t * tm, (t + 1) * tm) for t in range(SUBTILES)]


def _time_major(x):
    n, w = x.shape
    return jnp.swapaxes(x.reshape(SUBLANES, n // SUBLANES, w), 0, 1).reshape(n, w)


def _segment_major(x):
    n, w = x.shape
    return jnp.swapaxes(x.reshape(n // SUBLANES, SUBLANES, w), 0, 1).reshape(n, w)


def _band_start(j):
    return int(np.clip(j * NA_KW - NA_KW // 2, 0, GRID_W - NA_BAND))


def _ffn_proj_kernel(*refs, split):
    n_in = 1 if split is None else 2
    g1_ref, wup_ref, wdn_ref, g2_ref, gm_ref, win_ref = refs[n_in:n_in + 6]
    x1_ref, qcb_ref, kband_ref, vband_ref, xlru_ref, glru_ref, qca_ref = refs[n_in + 6:]
    toks = _subtiles(x1_ref.shape[1])
    if split is None:
        xs = [refs[0][0, tok, :] for tok in toks]
    else:
        first = pl.program_id(0) < split
        xs = [jnp.where(first, refs[0][0, tok, :], refs[1][0, tok, :]) for tok in toks]
    x1s = _swiglu_half_step(xs, g1_ref[...], wup_ref, wdn_ref, g2_ref[...])
    for tok, x1 in zip(toks, x1s):
        x1_ref[0, tok, :] = x1
    hs = [_rms(x1, gm_ref[...]).astype(BF16) for x1 in x1s]
    o = 3 * NA_W
    xlru_ref[0] = _time_major(jnp.concatenate([_dot(h, win_ref[:, o:o + LRU_W]) for h in hs], axis=0))
    for tok, h in zip(toks, hs):
        glru_ref[0, tok, :] = _dot(h, win_ref[:, o + LRU_W:o + 2 * LRU_W])
    for t, (tok, h) in enumerate(zip(toks, hs)):
        qkv = _dot(h, win_ref[:, 0:o])
        rows = qkv.shape[0] // GRID_W
        q = (qkv[:, 0:NA_W] * (NA_HEAD_DIM ** -0.5)).reshape(rows, GRID_W, NA_W)
        k = qkv[:, NA_W:2 * NA_W].reshape(rows, GRID_W, NA_W)
        v = qkv[:, 2 * NA_W:3 * NA_W].reshape(rows, GRID_W, NA_W)
        for j in range(NA_N_CB):
            bs = _band_start(j)
            qs = slice(t * rows * NA_KW, (t + 1) * rows * NA_KW)
            ks = slice(t * rows * NA_BAND, (t + 1) * rows * NA_BAND)
            qcb_ref[0, j, qs, :] = q[:, j * NA_KW:(j + 1) * NA_KW, :].reshape(rows * NA_KW, NA_W).astype(BF16)
            kband_ref[0, j, ks, :] = k[:, bs:bs + NA_BAND, :].reshape(rows * NA_BAND, NA_W).astype(BF16)
            vband_ref[0, j, ks, :] = v[:, bs:bs + NA_BAND, :].reshape(rows * NA_BAND, NA_W).astype(BF16)
        qca_ref[0, tok, :] = _dot(h, win_ref[:, o + 2 * LRU_W:o + 2 * LRU_W + CA_W]).astype(BF16)


def _resident(shape, layer):
    nd = len(shape)
    return pl.BlockSpec((None,) + tuple(shape), lambda *_: (layer,) + (0,) * nd,
                        pipeline_mode=pl.Buffered(1))


def _group_specs(block, batches, n_tiles):
    b0 = batches[0]
    return [
        pl.BlockSpec(block, lambda b, i: (jnp.minimum(b, b0 - 1), jnp.where(b < b0, i, n_tiles - 1), 0)),
        pl.BlockSpec(block, lambda b, i: (jnp.maximum(b - b0, 0), jnp.where(b >= b0, i, 0), 0)),
    ]


def _ffn_proj_call(layer, xs, g1, wup, wdn, g2, gm, win):
    T, D = xs[0].shape[1:]
    B = sum(x.shape[0] for x in xs)
    tm = TOKEN_TILE
    tok = lambda w: pl.BlockSpec((1, tm, w), lambda b, i: (b, i, 0))
    if len(xs) == 1:
        x_specs, split = [tok(D)], None
    else:
        x_specs, split = _group_specs((1, tm, D), [x.shape[0] for x in xs], T // tm), xs[0].shape[0]
    out_shape = (
        jax.ShapeDtypeStruct((B, T, D), F32),
        jax.ShapeDtypeStruct((B, NA_N_CB, T // NA_N_CB, NA_W), BF16),
        jax.ShapeDtypeStruct((B, NA_N_CB, T // 2, NA_W), BF16),
        jax.ShapeDtypeStruct((B, NA_N_CB, T // 2, NA_W), BF16),
        jax.ShapeDtypeStruct((B, T, LRU_W), F32),
        jax.ShapeDtypeStruct((B, T, LRU_W), F32),
        jax.ShapeDtypeStruct((B, T, CA_W), BF16),
    )
    out_specs = (
        tok(D),
        pl.BlockSpec((1, NA_N_CB, tm // NA_N_CB, NA_W), lambda b, i: (b, 0, i, 0)),
        pl.BlockSpec((1, NA_N_CB, tm // 2, NA_W), lambda b, i: (b, 0, i, 0)),
        pl.BlockSpec((1, NA_N_CB, tm // 2, NA_W), lambda b, i: (b, 0, i, 0)),
        tok(LRU_W), tok(LRU_W), tok(CA_W),
    )
    in_specs = x_specs + [_resident(w.shape[1:], layer) for w in (g1, wup, wdn, g2, gm, win)]
    return pl.pallas_call(
        functools.partial(_ffn_proj_kernel, split=split),
        grid=(B, T // tm),
        in_specs=in_specs,
        out_specs=out_specs,
        out_shape=out_shape,
        compiler_params=pltpu.CompilerParams(
            dimension_semantics=("arbitrary", "arbitrary"), vmem_limit_bytes=VMEM_LIMIT_BYTES),
        name="ffn_proj",
    )(*xs, g1, wup, wdn, g2, gm, win)


def _mem_kv_kernel(mem_ref, g_ref, w_ref, kv_ref):
    h = _rms(mem_ref[0], g_ref[...]).astype(BF16)
    kv_ref[0, 0] = _dot(h, w_ref[...]).astype(BF16)


def _mem_kv_call(mem, g_mem, w_kv):
    B, M, D = mem.shape
    L = w_kv.shape[0]
    return pl.pallas_call(
        _mem_kv_kernel,
        grid=(L, B),
        in_specs=[
            pl.BlockSpec((1, M, D), lambda l, b: (b, 0, 0)),
            pl.BlockSpec((None, 1, D), lambda l, b: (l, 0, 0)),
            pl.BlockSpec((None, D, 2 * CA_W), lambda l, b: (l, 0, 0)),
        ],
        out_specs=pl.BlockSpec((1, 1, M, 2 * CA_W), lambda l, b: (l, b, 0, 0)),
        out_shape=jax.ShapeDtypeStruct((L, B, M, 2 * CA_W), BF16),
        compiler_params=pltpu.CompilerParams(dimension_semantics=("arbitrary", "arbitrary")),
        name="mem_kv",
    )(mem, g_mem, w_kv)


def _na_key_row_start(i, rows):
    return jnp.clip(i * NA_QROWS - NA_KH // 2, 0, rows - NA_KROWS)


def _na_bias_tables(rpb, rows):
    L, H = rpb.shape[:2]
    n_blocks = rows // NA_QROWS
    n_dr = 2 * NA_KH - 1
    j = np.arange(NA_N_CB)[:, None, None]
    qc = j * NA_KW + np.arange(NA_KW)[None, :, None]
    bs = np.clip(j * NA_KW - NA_KW // 2, 0, GRID_W - NA_BAND)
    kc = bs + np.arange(NA_BAND)[None, None, :]
    ws = np.clip(qc - NA_KW // 2, 0, GRID_W - NA_KW)
    col_ok = (kc >= ws) & (kc < ws + NA_KW)
    rp = jnp.pad(rpb, ((0, 0), (0, 0), (0, 0), (NA_BAND, NA_BAND)))
    cols = []
    for jj in range(NA_N_CB):
        for q in range(NA_KW):
            shift = NA_BAND + int(bs[jj, 0, 0]) - (jj * NA_KW + q) + (NA_KW - 1)
            cols.append(rp[..., shift:shift + NA_BAND])
    t1 = jnp.stack(cols).reshape(NA_N_CB, NA_KW, L, H, n_dr, NA_BAND)
    t1 = jnp.where(jnp.asarray(col_ok)[:, :, None, None, None, :], t1, NEG)
    t1 = t1.reshape(NA_N_CB, NA_KW, L, H, n_dr * NA_BAND)
    runs = []
    for i in (0, min(1, n_blocks - 1), n_blocks - 1):
        ks = int(np.clip(i * NA_QROWS - NA_KH // 2, 0, rows - NA_KROWS))
        for aa in range(NA_QROWS):
            r = i * NA_QROWS + aa
            rs = int(np.clip(r - NA_KH // 2, 0, rows - NA_KH))
            lo = rs - ks
            dr0 = rs - r + NA_KH - 1
            assert 0 <= lo <= NA_KROWS - NA_KH and 0 <= dr0 <= n_dr - NA_KH
            run = t1[..., dr0 * NA_BAND:(dr0 + NA_KH) * NA_BAND]
            runs.append(jnp.pad(run, ((0, 0),) * 4 + ((lo * NA_BAND, (NA_KROWS - NA_KH - lo) * NA_BAND),),
                                constant_values=NEG))
    nk = NA_KROWS * NA_BAND
    bias = jnp.stack(runs).reshape(3, NA_QROWS, NA_N_CB, NA_KW, L, H, nk)
    bias = jnp.transpose(bias, (4, 0, 5, 2, 1, 3, 6))
    return bias.reshape(L, 3, H, NA_N_CB, NA_QROWS * NA_KW, nk)


def _na_kernel(q_ref, k_ref, v_ref, bias_ref, o_ref, *, rows):
    step = pl.program_id(2)
    n_blocks = rows // NA_QROWS
    nq = NA_QROWS * NA_KW
    nk = NA_KROWS * NA_BAND
    lane = lax.broadcasted_iota(jnp.int32, (nq, LANES), 1)
    first_head = lane < NA_HEAD_DIM
    units = [(sub, j) for sub in range(NA_BLOCKS_PER_STEP) for j in range(NA_N_CB)]

    def scores(sub, j):
        i = step * NA_BLOCKS_PER_STEP + sub
        ks = pl.multiple_of(_na_key_row_start(i, rows) * NA_BAND, NA_BAND * (NA_KH // 2))
        variant = jnp.where(i == 0, 0, jnp.where(i == n_blocks - 1, 2, 1))
        q = q_ref[0, j, sub * nq:(sub + 1) * nq, :]
        k = k_ref[0, j, pl.ds(ks, nk), :]
        zero = jnp.zeros_like(q)
        q2 = jnp.concatenate([jnp.where(first_head, q, zero), jnp.where(first_head, zero, q)], axis=0)
        return _dot_nt(q2, k) + bias_ref[variant, :, j].reshape(NA_HEADS_PER_STEP * nq, nk), ks

    def softmax(s):
        m = jnp.max(s, axis=-1, keepdims=True)
        p = jnp.exp(s - m)
        return p.astype(BF16), jnp.sum(p, axis=-1, keepdims=True)

    def finish(sub, j, p, l, ks):
        v = v_ref[0, j, pl.ds(ks, nk), :]
        o2 = _dot(p, v) / l
        o = jnp.where(first_head, o2[:nq], o2[nq:]).astype(BF16)
        for a in range(NA_QROWS):
            row0 = (sub * NA_QROWS + a) * GRID_W + j * NA_KW
            o_ref[0, row0:row0 + NA_KW, :] = o[a * NA_KW:(a + 1) * NA_KW, :]

    n = len(units)
    sc, pr = {}, {}
    for t in range(n + 2):
        if t < n:
            sc[t] = scores(*units[t])
        if 0 <= t - 1 < n:
            s, ks = sc.pop(t - 1)
            pr[t - 1] = softmax(s) + (ks,)
        if 0 <= t - 2 < n:
            finish(*units[t - 2], *pr.pop(t - 2))


def _na_call(layer, q_cb, k_band, v_band, bias):
    B, _, tq, _ = q_cb.shape
    T = tq * NA_N_CB
    rows = T // GRID_W
    n_steps = rows // (NA_QROWS * NA_BLOCKS_PER_STEP)
    n_pairs = NA_HEADS // NA_HEADS_PER_STEP
    nq = NA_QROWS * NA_KW
    nk = NA_KROWS * NA_BAND
    kv_spec = pl.BlockSpec((1, NA_N_CB, T // 2, LANES), lambda hp, b, i: (b, 0, 0, hp))
    return pl.pallas_call(
        functools.partial(_na_kernel, rows=rows),
        grid=(n_pairs, B, n_steps),
        in_specs=[
            pl.BlockSpec((1, NA_N_CB, nq * NA_BLOCKS_PER_STEP, LANES), lambda hp, b, i: (b, 0, i, hp)),
            kv_spec, kv_spec,
            pl.BlockSpec((None, 3, NA_HEADS_PER_STEP, NA_N_CB, nq, nk),
                         lambda hp, b, i: (layer, 0, hp, 0, 0, 0), pipeline_mode=pl.Buffered(1)),
        ],
        out_specs=pl.BlockSpec((1, NA_QROWS * NA_BLOCKS_PER_STEP * GRID_W, LANES), lambda hp, b, i: (b, i, hp)),
        out_shape=jax.ShapeDtypeStruct((B, T, NA_W), BF16),
        compiler_params=pltpu.CompilerParams(
            dimension_semantics=("arbitrary", "arbitrary", "arbitrary"), vmem_limit_bytes=VMEM_LIMIT_BYTES),
        name="na_attn",
    )(q_cb, k_band, v_band, bias)


def _lru_gate_weights(wa, wi):
    L = wa.shape[0]
    ng = LRU_BLOCKS // LRU_GROUP

    def blockdiag(w):
        w = w.reshape(L, 2, ng, LRU_GROUP, LRU_BW, LRU_BW)
        eye = jnp.eye(LRU_GROUP, dtype=w.dtype)
        full = w[:, :, :, :, :, None, :] * eye[:, None, :, None]
        return full.reshape(L, 2, ng, LANES, LANES)

    return jnp.concatenate([blockdiag(wa), blockdiag(wi)], axis=-1).astype(BF16)


def _lru_gate_bias(ba, bi):
    L = ba.shape[0]
    ng = LRU_W // LANES
    return jnp.concatenate([ba.reshape(L, 2, ng, 1, LANES), bi.reshape(L, 2, ng, 1, LANES)], axis=-1)


def _softplus(x):
    return jnp.maximum(x, 0.0) + jnp.log1p(jnp.exp(-jnp.abs(x)))


def _lru_inputs(prev_ref, x_ref, next_ref, has_prev, has_next, cw_ref, cb_ref, w_ref, b_ref, lam_ref, d,
                a_scr, u_scr):
    assert CONV_W == 4
    ct, w = x_ref.shape[1], x_ref.shape[2]
    nk = ct // SUBLANES
    x = x_ref[0].reshape(nk, SUBLANES, w)
    sub = lax.broadcasted_iota(jnp.int32, (SUBLANES, w), 0)
    prev = jnp.where(has_prev, prev_ref[0], 0.0)
    nxt = jnp.where(has_next, next_ref[0], 0.0)
    back1 = jnp.where(sub == 0, pltpu.roll(prev[SUBLANES:], 1, 0), pltpu.roll(x[nk - 1], 1, 0))
    back2 = jnp.where(sub == 0, pltpu.roll(prev[:SUBLANES], 1, 0), pltpu.roll(x[nk - 2], 1, 0))
    fwd1 = jnp.where(sub == SUBLANES - 1, pltpu.roll(nxt, SUBLANES - 1, 0), pltpu.roll(x[0], SUBLANES - 1, 0))
    taps = (
        jnp.concatenate([back2[None], back1[None], x[:nk - 2]], axis=0),
        jnp.concatenate([back1[None], x[:nk - 1]], axis=0),
        x,
        jnp.concatenate([x[1:], fwd1[None]], axis=0),
    )
    xc = cb_ref[...]
    for t in range(CONV_W):
        xc = xc + taps[t] * cw_ref[t:t + 1, :]
    xc = xc.reshape(ct, w)
    xcb = xc.astype(BF16)
    sp = _softplus(-lam_ref[d:d + 1, :])
    for g in range(LRU_W // LANES):
        sl = slice(g * LANES, (g + 1) * LANES)
        z = _dot(xcb[:, sl], w_ref[d, g]) + b_ref[d, g]
        r = jax.nn.sigmoid(z[:, :LANES])
        gi = jax.nn.sigmoid(z[:, LANES:])
        log_a = (-LRU_C) * r * sp[:, sl]
        a = jnp.exp(log_a)
        y = -jnp.tanh(log_a) * (a * a + 1.0)
        u = (y * lax.rsqrt(jnp.maximum(y, F32_TINY))) * (gi * xc[:, sl])
        a_scr[:, sl] = a
        u_scr[:, sl] = u


def _slab_scan(a, b, reverse):
    row = lax.broadcasted_iota(jnp.int32, a.shape, 0)
    for sh in (1, 2, 4):
        if reverse:
            ok = row < SUBLANES - sh
            a_sh = pltpu.roll(a, SUBLANES - sh, 0)
            b_sh = pltpu.roll(b, SUBLANES - sh, 0)
        else:
            ok = row >= sh
            a_sh = pltpu.roll(a, sh, 0)
            b_sh = pltpu.roll(b, sh, 0)
        b = a * jnp.where(ok, b_sh, 0.0) + b
        a = a * jnp.where(ok, a_sh, 1.0)
    return a, b


def _segment_scan(a_scr, u_scr, h_ref, carry_scr, reverse):
    ct, w = a_scr.shape
    nk = ct // SUBLANES

    def group(k):
        kk = (nk - 1 - k) if reverse else k
        return slice(kk * SUBLANES, (kk + 1) * SUBLANES)

    h = jnp.zeros((SUBLANES, w), F32)
    acc = jnp.ones((SUBLANES, w), F32)
    for k in range(nk):
        rows = group(k)
        a = a_scr[rows, :]
        h = a * h + u_scr[rows, :]
        acc = a * acc
        h_ref[0, rows, :] = h
        a_scr[rows, :] = acc
    a_inc, h_inc = _slab_scan(acc, h, reverse)
    state = h_inc + a_inc * carry_scr[...]
    sub = lax.broadcasted_iota(jnp.int32, (SUBLANES, w), 0)
    if reverse:
        carry_out = state[0:1, :]
        incoming = jnp.where(sub == SUBLANES - 1, carry_scr[...], pltpu.roll(state, SUBLANES - 1, 0))
    else:
        carry_out = state[SUBLANES - 1:SUBLANES, :]
        incoming = jnp.where(sub == 0, carry_scr[...], pltpu.roll(state, 1, 0))
    carry_scr[...] = carry_out
    for k in range(nk):
        rows = group(k)
        h_ref[0, rows, :] = h_ref[0, rows, :] + a_scr[rows, :] * incoming


def _lru_kernel(xf_ref, pf_ref, nf_ref, xb_ref, pb_ref, nb_ref, cw_ref, cb_ref, w_ref, b_ref, lam_ref,
                hf_ref, hb_ref, af_scr, uf_scr, ab_scr, ub_scr, cf_scr, cbk_scr):
    c = pl.program_id(1)
    nc = pl.num_programs(1)

    @pl.when(c == 0)
    def _():
        cf_scr[...] = jnp.zeros_like(cf_scr)
        cbk_scr[...] = jnp.zeros_like(cbk_scr)

    _lru_inputs(pf_ref, xf_ref, nf_ref, c > 0, c < nc - 1, cw_ref, cb_ref, w_ref, b_ref, lam_ref, 0,
                af_scr, uf_scr)
    _lru_inputs(pb_ref, xb_ref, nb_ref, c < nc - 1, c > 0, cw_ref, cb_ref, w_ref, b_ref, lam_ref, 1,
                ab_scr, ub_scr)
    _segment_scan(af_scr, uf_scr, hf_ref, cf_scr, False)
    _segment_scan(ab_scr, ub_scr, hb_ref, cbk_scr, True)


def _lru_call(layer, x_lru, conv_w, conv_b, w_gate, b_gate, lam):
    B, T, W = x_lru.shape
    ct = LRU_CHUNK
    nc = T // ct
    tail = 2 * SUBLANES
    n_tail = ct // tail
    n_head = ct // SUBLANES

    main_f = pl.BlockSpec((1, ct, W), lambda b, c: (b, c, 0))
    prev_f = pl.BlockSpec((1, tail, W), lambda b, c: (b, jnp.maximum(c * n_tail - 1, 0), 0))
    next_f = pl.BlockSpec((1, SUBLANES, W), lambda b, c: (b, jnp.minimum(c + 1, nc - 1) * n_head, 0))
    main_b = pl.BlockSpec((1, ct, W), lambda b, c: (b, nc - 1 - c, 0))
    prev_b = pl.BlockSpec((1, tail, W), lambda b, c: (b, jnp.maximum((nc - 1 - c) * n_tail - 1, 0), 0))
    next_b = pl.BlockSpec((1, SUBLANES, W), lambda b, c: (b, jnp.minimum(nc - c, nc - 1) * n_head, 0))
    params = (conv_w, conv_b, w_gate, b_gate, lam)
    return pl.pallas_call(
        _lru_kernel,
        grid=(B, nc),
        in_specs=[main_f, prev_f, next_f, main_b, prev_b, next_b] + [_resident(p.shape[1:], layer) for p in params],
        out_specs=(main_f, main_b),
        out_shape=(jax.ShapeDtypeStruct((B, T, W), F32), jax.ShapeDtypeStruct((B, T, W), F32)),
        scratch_shapes=[pltpu.VMEM((ct, W), F32)] * 4 + [pltpu.VMEM((1, W), F32)] * 2,
        compiler_params=pltpu.CompilerParams(dimension_semantics=("arbitrary", "arbitrary")),
        name="rg_lru",
    )(x_lru, x_lru, x_lru, x_lru, x_lru, x_lru, *params)


def _merge_ffn_kernel(x_ref, yna_ref, hf_ref, hb_ref, glru_ref, qca_ref, kv_ref,
                      gm_ref, wg_ref, bg_ref, wna_ref, wlru_ref, wca_ref, wout_ref, gpost_ref,
                      g1_ref, wup_ref, wdn_ref, g2_ref, *o_refs, split):
    toks = _subtiles(x_ref.shape[1])
    d = x_ref.shape[-1]
    x1s = [x_ref[0, tok, :] for tok in toks]
    hs = [_rms(x1, gm_ref[...]).astype(BF16) for x1 in x1s]

    heads = [[] for _ in toks]
    for hd in range(CA_HEADS):
        sl = slice(hd * CA_HEAD_DIM, (hd + 1) * CA_HEAD_DIM)
        k = kv_ref[0, 0, :, hd * CA_HEAD_DIM:(hd + 1) * CA_HEAD_DIM]
        v = kv_ref[0, 0, :, CA_W + hd * CA_HEAD_DIM:CA_W + (hd + 1) * CA_HEAD_DIM]
        for t, tok in enumerate(toks):
            s = _dot_nt(qca_ref[0, tok, sl], k) * (CA_HEAD_DIM ** -0.5)
            m = jnp.max(s, axis=-1, keepdims=True)
            p = jnp.exp(s - m)
            l = jnp.sum(p, axis=-1, keepdims=True)
            heads[t].append((_dot(p.astype(BF16), v) / l).astype(BF16))
    y_cas = [jnp.concatenate(hh, axis=-1) for hh in heads]

    def gate(h, n):
        return jax.nn.sigmoid(_dot(h, wg_ref[:, n * d:(n + 1) * d]) + bg_ref[:, n * d:(n + 1) * d])

    merged = [gate(h, 0) * _dot(yna_ref[0, tok, :], wna_ref[...]) for tok, h in zip(toks, hs)]
    h_lru = _segment_major(hf_ref[0] + hb_ref[0])
    merged = [mg + gate(h, 1) * _dot((h_lru[tok, :] * jax.nn.gelu(glru_ref[0, tok, :])).astype(BF16), wlru_ref[...])
              for tok, h, mg in zip(toks, hs, merged)]
    merged = [mg + gate(h, 2) * _dot(y_ca, wca_ref[...]) for h, mg, y_ca in zip(hs, merged, y_cas)]
    x2s = [x1 + _rms(_dot(mg.astype(BF16), wout_ref[...]), gpost_ref[...]) for x1, mg in zip(x1s, merged)]
    outs = _swiglu_half_step(x2s, g1_ref[...], wup_ref, wdn_ref, g2_ref[...])

    def store(o_ref):
        for tok, out in zip(toks, outs):
            o_ref[0, tok, :] = out

    if split is None:
        store(o_refs[0])
    else:
        first = pl.program_id(0) < split
        pl.when(first)(lambda: store(o_refs[0]))
        pl.when(jnp.logical_not(first))(lambda: store(o_refs[1]))


def _merge_ffn_call(layer, x1, y_na, h_f, h_b, g_lru, q_ca, kv, params, out_batches=None):
    B, T, D = x1.shape
    M = kv.shape[2]
    tm = TOKEN_TILE
    tok = lambda w: pl.BlockSpec((1, tm, w), lambda b, i: (b, i, 0))
    in_specs = [tok(D), tok(NA_W), tok(LRU_W), tok(LRU_W), tok(LRU_W), tok(CA_W),
                pl.BlockSpec((1, 1, M, 2 * CA_W), lambda b, i: (layer, b, 0, 0))]
    in_specs += [_resident(p.shape[1:], layer) for p in params]
    if out_batches is None:
        out_specs, out_shape, split = tok(D), jax.ShapeDtypeStruct((B, T, D), F32), None
    else:
        assert sum(out_batches) == B
        out_specs = tuple(_group_specs((1, tm, D), out_batches, T // tm))
        out_shape = tuple(jax.ShapeDtypeStruct((n, T, D), F32) for n in out_batches)
        split = out_batches[0]
    return pl.pallas_call(
        functools.partial(_merge_ffn_kernel, split=split),
        grid=(B, T // tm),
        in_specs=in_specs,
        out_specs=out_specs,
        out_shape=out_shape,
        compiler_params=pltpu.CompilerParams(
            dimension_semantics=("arbitrary", "arbitrary"), vmem_limit_bytes=VMEM_LIMIT_BYTES),
        name="merge_ffn",
    )(x1, y_na, h_f, h_b, g_lru, q_ca, kv, *params)


def kernel(x_prompt, x_sample, mem_prompt, mem_sample, g_ffn1_pre, w_ffn1_up, w_ffn1_down, g_ffn1_post, g_mix_pre, w_in, na_rpb, conv_w, conv_b, lru_wa, lru_ba, lru_wi, lru_bi, lru_lambda, g_mem, w_mem_kv, w_gate, b_gate, w_branch_na, w_branch_lru, w_branch_ca, w_out, g_mix_post, g_ffn2_pre, w_ffn2_up, w_ffn2_down, g_ffn2_post):
    assert x_prompt.shape[1:] == x_sample.shape[1:] and mem_prompt.shape[1:] == mem_sample.shape[1:]
    batches = (x_prompt.shape[0], x_sample.shape[0])
    mem = jnp.concatenate([mem_prompt, mem_sample], axis=0)
    T = x_prompt.shape[1]
    L = w_in.shape[0]
    rows = T // GRID_W
    assert rows % (NA_QROWS * NA_BLOCKS_PER_STEP) == 0 and rows >= NA_KROWS
    assert T % TOKEN_TILE == 0 and LRU_CHUNK == TOKEN_TILE
    assert w_ffn1_down.shape[1] % MXU_DIM == 0 and w_ffn2_down.shape[1] % MXU_DIM == 0

    bf = lambda w: w.astype(BF16)
    vec = lambda g: g.reshape(L, 1, g.shape[-1])
    w1u, w1d, w2u, w2d = bf(w_ffn1_up), bf(w_ffn1_down), bf(w_ffn2_up), bf(w_ffn2_down)
    win, wg, wout, wkv = bf(w_in), bf(w_gate), bf(w_out), bf(w_mem_kv)
    wna, wlru, wca = bf(w_branch_na), bf(w_branch_lru), bf(w_branch_ca)
    lru_w = _lru_gate_weights(lru_wa, lru_wi)
    lru_b = _lru_gate_bias(lru_ba, lru_bi)
    bias = _na_bias_tables(na_rpb, rows)
    kv = _mem_kv_call(mem, vec(g_mem), wkv)

    xs = (x_prompt, x_sample)
    for l in range(L):
        x1, q_cb, k_band, v_band, x_lru, g_lru, q_ca = _ffn_proj_call(
            l, xs, vec(g_ffn1_pre), w1u, w1d, vec(g_ffn1_post), vec(g_mix_pre), win)
        y_na = _na_call(l, q_cb, k_band, v_band, bias)
        h_f, h_b = _lru_call(l, x_lru, conv_w, vec(conv_b), lru_w, lru_b, lru_lambda)
        out = _merge_ffn_call(
            l, x1, y_na, h_f, h_b, g_lru, q_ca, kv,
            (vec(g_mix_pre), wg, vec(b_gate), wna, wlru, wca, wout, vec(g_mix_post),
             vec(g_ffn2_pre), w2u, w2d, vec(g_ffn2_post)),
            out_batches=batches if l == L - 1 else None)
        xs = (out,)
    return tuple(out)
```

```python
import functools

import numpy as np
import jax
import jax.numpy as jnp
from jax import lax
from jax.experimental import pallas as pl
from jax.experimental.pallas import tpu as pltpu

GRID_W = 64
NA_HEADS = 8
NA_HEAD_DIM = 64
NA_W = NA_HEADS * NA_HEAD_DIM
NA_KH = 8
NA_KW = 16
NA_N_CB = GRID_W // NA_KW
NA_BAND = 2 * NA_KW
LRU_W = 512
LRU_BLOCKS = 8
LRU_BW = LRU_W // LRU_BLOCKS
CONV_W = 4
LRU_C = 8.0
CA_HEADS = 4
CA_HEAD_DIM = 128
CA_W = CA_HEADS * CA_HEAD_DIM
EPS = 1e-6
NEG = -1e30
F32_TINY = float(np.finfo(np.float32).tiny)

LANES = 128
SUBLANES = 8
MXU_DIM = 256
VMEM_BYTES = 64 * 1024 * 1024
VMEM_LIMIT_BYTES = VMEM_BYTES - 4 * 1024 * 1024

TOKEN_TILE = 512
SUBTILES = 2
FF_CHUNK_TILES = 6
NA_QROWS = 8
NA_KROWS = 2 * NA_KH
NA_HEADS_PER_STEP = LANES // NA_HEAD_DIM
NA_BLOCKS_PER_STEP = 4
LRU_CHUNK = 512
LRU_GROUP = LANES // LRU_BW

BF16 = jnp.bfloat16
F32 = jnp.float32


def _rms(x, g):
    return x * lax.rsqrt(jnp.mean(x * x, axis=-1, keepdims=True) + EPS) * g


def _dot(a, b):
    return jnp.dot(a, b, preferred_element_type=F32)


def _dot_nt(a, b):
    return lax.dot_general(a, b, (((1,), (1,)), ((), ())), preferred_element_type=F32)


def _ff_chunks(d_ff):
    tiles = d_ff // MXU_DIM
    widths = []
    while tiles > 0:
        n = min(FF_CHUNK_TILES, tiles)
        widths.append(n * MXU_DIM)
        tiles -= n
    return widths


def _swiglu_half_step(xs, g_pre, w_up_ref, w_down_ref, g_post):
    d_ff = w_down_ref.shape[0]
    hs = [_rms(x, g_pre).astype(BF16) for x in xs]
    accs = [None] * len(xs)
    lo = 0
    for cw in _ff_chunks(d_ff):
        for t, h in enumerate(hs):
            a = _dot(h, w_up_ref[:, lo:lo + cw])
            b = _dot(h, w_up_ref[:, d_ff + lo:d_ff + lo + cw])
            act = (a * jax.nn.sigmoid(a) * b).astype(BF16)
            part = _dot(act, w_down_ref[lo:lo + cw, :])
            accs[t] = part if accs[t] is None else accs[t] + part
        lo += cw
    return [x + 0.5 * _rms(acc, g_post) for x, acc in zip(xs, accs)]


def _subtiles(n_tokens):
    tm = n_tokens // SUBTILES
    return [slice(t * tm, (t + 1) * tm) for t in range(SUBTILES)]


def _time_major(x):
    n, w = x.shape
    return jnp.swapaxes(x.reshape(SUBLANES, n // SUBLANES, w), 0, 1).reshape(n, w)


def _segment_major(x):
    n, w = x.shape
    return jnp.swapaxes(x.reshape(n // SUBLANES, SUBLANES, w), 0, 1).reshape(n, w)


def _band_start(j):
    return int(np.clip(j * NA_KW - NA_KW // 2, 0, GRID_W - NA_BAND))


def _ffn_proj_kernel(*refs, split):
    n_in = 1 if split is None else 2
    g1_ref, wup_ref, wdn_ref, g2_ref, gm_ref, win_ref = refs[n_in:n_in + 6]
    x1_ref, qcb_ref, kband_ref, vband_ref, xlru_ref, glru_ref, qca_ref = refs[n_in + 6:]
    toks = _subtiles(x1_ref.shape[1])
    if split is None:
        xs = [refs[0][0, tok, :] for tok in toks]
    else:
        first = pl.program_id(0) < split
        xs = [jnp.where(first, refs[0][0, tok, :], refs[1][0, tok, :]) for tok in toks]
    x1s = _swiglu_half_step(xs, g1_ref[...], wup_ref, wdn_ref, g2_ref[...])
    for tok, x1 in zip(toks, x1s):
        x1_ref[0, tok, :] = x1
    hs = [_rms(x1, gm_ref[...]).astype(BF16) for x1 in x1s]
    o = 3 * NA_W
    xlru_ref[0] = _time_major(jnp.concatenate([_dot(h, win_ref[:, o:o + LRU_W]) for h in hs], axis=0))
    for tok, h in zip(toks, hs):
        glru_ref[0, tok, :] = _dot(h, win_ref[:, o + LRU_W:o + 2 * LRU_W])
    for t, (tok, h) in enumerate(zip(toks, hs)):
        qkv = _dot(h, win_ref[:, 0:o])
        rows = qkv.shape[0] // GRID_W
        q = (qkv[:, 0:NA_W] * (NA_HEAD_DIM ** -0.5)).reshape(rows, GRID_W, NA_W)
        k = qkv[:, NA_W:2 * NA_W].reshape(rows, GRID_W, NA_W)
        v = qkv[:, 2 * NA_W:3 * NA_W].reshape(rows, GRID_W, NA_W)
        for j in range(NA_N_CB):
            bs = _band_start(j)
            qs = slice(t * rows * NA_KW, (t + 1) * rows * NA_KW)
            ks = slice(t * rows * NA_BAND, (t + 1) * rows * NA_BAND)
            qcb_ref[0, j, qs, :] = q[:, j * NA_KW:(j + 1) * NA_KW, :].reshape(rows * NA_KW, NA_W).astype(BF16)
            kband_ref[0, j, ks, :] = k[:, bs:bs + NA_BAND, :].reshape(rows * NA_BAND, NA_W).astype(BF16)
            vband_ref[0, j, ks, :] = v[:, bs:bs + NA_BAND, :].reshape(rows * NA_BAND, NA_W).astype(BF16)
        qca_ref[0, tok, :] = _dot(h, win_ref[:, o + 2 * LRU_W:o + 2 * LRU_W + CA_W]).astype(BF16)


def _resident(shape, layer):
    nd = len(shape)
    return pl.BlockSpec((None,) + tuple(shape), lambda *_: (layer,) + (0,) * nd,
                        pipeline_mode=pl.Buffered(1))


def _group_specs(block, batches, n_tiles):
    b0 = batches[0]
    return [
        pl.BlockSpec(block, lambda b, i: (jnp.minimum(b, b0 - 1), jnp.where(b < b0, i, n_tiles - 1), 0)),
        pl.BlockSpec(block, lambda b, i: (jnp.maximum(b - b0, 0), jnp.where(b >= b0, i, 0), 0)),
    ]


def _ffn_proj_call(layer, xs, g1, wup, wdn, g2, gm, win):
    T, D = xs[0].shape[1:]
    B = sum(x.shape[0] for x in xs)
    tm = TOKEN_TILE
    tok = lambda w: pl.BlockSpec((1, tm, w), lambda b, i: (b, i, 0))
    if len(xs) == 1:
        x_specs, split = [tok(D)], None
    else:
        x_specs, split = _group_specs((1, tm, D), [x.shape[0] for x in xs], T // tm), xs[0].shape[0]
    out_shape = (
        jax.ShapeDtypeStruct((B, T, D), F32),
        jax.ShapeDtypeStruct((B, NA_N_CB, T // NA_N_CB, NA_W), BF16),
        jax.ShapeDtypeStruct((B, NA_N_CB, T // 2, NA_W), BF16),
        jax.ShapeDtypeStruct((B, NA_N_CB, T // 2, NA_W), BF16),
        jax.ShapeDtypeStruct((B, T, LRU_W), F32),
        jax.ShapeDtypeStruct((B, T, LRU_W), F32),
        jax.ShapeDtypeStruct((B, T, CA_W), BF16),
    )
    out_specs = (
        tok(D),
        pl.BlockSpec((1, NA_N_CB, tm // NA_N_CB, NA_W), lambda b, i: (b, 0, i, 0)),
        pl.BlockSpec((1, NA_N_CB, tm // 2, NA_W), lambda b, i: (b, 0, i, 0)),
        pl.BlockSpec((1, NA_N_CB, tm // 2, NA_W), lambda b, i: (b, 0, i, 0)),
        tok(LRU_W), tok(LRU_W), tok(CA_W),
    )
    in_specs = x_specs + [_resident(w.shape[1:], layer) for w in (g1, wup, wdn, g2, gm, win)]
    return pl.pallas_call(
        functools.partial(_ffn_proj_kernel, split=split),
        grid=(B, T // tm),
        in_specs=in_specs,
        out_specs=out_specs,
        out_shape=out_shape,
        compiler_params=pltpu.CompilerParams(
            dimension_semantics=("arbitrary", "arbitrary"), vmem_limit_bytes=VMEM_LIMIT_BYTES),
        name="ffn_proj",
    )(*xs, g1, wup, wdn, g2, gm, win)


def _mem_kv_kernel(mem_ref, g_ref, w_ref, kv_ref):
    h = _rms(mem_ref[0], g_ref[...]).astype(BF16)
    kv_ref[0, 0] = _dot(h, w_ref[...]).astype(BF16)


def _mem_kv_call(mem, g_mem, w_kv):
    B, M, D = mem.shape
    L = w_kv.shape[0]
    return pl.pallas_call(
        _mem_kv_kernel,
        grid=(L, B),
        in_specs=[
            pl.BlockSpec((1, M, D), lambda l, b: (b, 0, 0)),
            pl.BlockSpec((None, 1, D), lambda l, b: (l, 0, 0)),
            pl.BlockSpec((None, D, 2 * CA_W), lambda l, b: (l, 0, 0)),
        ],
        out_specs=pl.BlockSpec((1, 1, M, 2 * CA_W), lambda l, b: (l, b, 0, 0)),
        out_shape=jax.ShapeDtypeStruct((L, B, M, 2 * CA_W), BF16),
        compiler_params=pltpu.CompilerParams(dimension_semantics=("arbitrary", "arbitrary")),
        name="mem_kv",
    )(mem, g_mem, w_kv)


def _na_key_row_start(i, rows):
    return jnp.clip(i * NA_QROWS - NA_KH // 2, 0, rows - NA_KROWS)


def _na_bias_tables(rpb, rows):
    L, H = rpb.shape[:2]
    n_blocks = rows // NA_QROWS
    n_dr = 2 * NA_KH - 1
    j = np.arange(NA_N_CB)[:, None, None]
    qc = j * NA_KW + np.arange(NA_KW)[None, :, None]
    bs = np.clip(j * NA_KW - NA_KW // 2, 0, GRID_W - NA_BAND)
    kc = bs + np.arange(NA_BAND)[None, None, :]
    ws = np.clip(qc - NA_KW // 2, 0, GRID_W - NA_KW)
    col_ok = (kc >= ws) & (kc < ws + NA_KW)
    rp = jnp.pad(rpb, ((0, 0), (0, 0), (0, 0), (NA_BAND, NA_BAND)))
    cols = []
    for jj in range(NA_N_CB):
        for q in range(NA_KW):
            shift = NA_BAND + int(bs[jj, 0, 0]) - (jj * NA_KW + q) + (NA_KW - 1)
            cols.append(rp[..., shift:shift + NA_BAND])
    t1 = jnp.stack(cols).reshape(NA_N_CB, NA_KW, L, H, n_dr, NA_BAND)
    t1 = jnp.where(jnp.asarray(col_ok)[:, :, None, None, None, :], t1, NEG)
    t1 = t1.reshape(NA_N_CB, NA_KW, L, H, n_dr * NA_BAND)
    runs = []
    for i in (0, min(1, n_blocks - 1), n_blocks - 1):
        ks = int(np.clip(i * NA_QROWS - NA_KH // 2, 0, rows - NA_KROWS))
        for aa in range(NA_QROWS):
            r = i * NA_QROWS + aa
            rs = int(np.clip(r - NA_KH // 2, 0, rows - NA_KH))
            lo = rs - ks
            dr0 = rs - r + NA_KH - 1
            assert 0 <= lo <= NA_KROWS - NA_KH and 0 <= dr0 <= n_dr - NA_KH
            run = t1[..., dr0 * NA_BAND:(dr0 + NA_KH) * NA_BAND]
            runs.append(jnp.pad(run, ((0, 0),) * 4 + ((lo * NA_BAND, (NA_KROWS - NA_KH - lo) * NA_BAND),),
                                constant_values=NEG))
    nk = NA_KROWS * NA_BAND
    bias = jnp.stack(runs).reshape(3, NA_QROWS, NA_N_CB, NA_KW, L, H, nk)
    bias = jnp.transpose(bias, (4, 0, 5, 2, 1, 3, 6))
    return bias.reshape(L, 3, H, NA_N_CB, NA_QROWS * NA_KW, nk)


def _na_kernel(q_ref, k_ref, v_ref, bias_ref, o_ref, *, rows):
    step = pl.program_id(2)
    n_blocks = rows // NA_QROWS
    nq = NA_QROWS * NA_KW
    nk = NA_KROWS * NA_BAND
    lane = lax.broadcasted_iota(jnp.int32, (nq, LANES), 1)
    first_head = lane < NA_HEAD_DIM
    units = [(sub, j) for sub in range(NA_BLOCKS_PER_STEP) for j in range(NA_N_CB)]

    def scores(sub, j):
        i = step * NA_BLOCKS_PER_STEP + sub
        ks = pl.multiple_of(_na_key_row_start(i, rows) * NA_BAND, NA_BAND * (NA_KH // 2))
        variant = jnp.where(i == 0, 0, jnp.where(i == n_blocks - 1, 2, 1))
        q = q_ref[0, j, sub * nq:(sub + 1) * nq, :]
        k = k_ref[0, j, pl.ds(ks, nk), :]
        zero = jnp.zeros_like(q)
        q2 = jnp.concatenate([jnp.where(first_head, q, zero), jnp.where(first_head, zero, q)], axis=0)
        return _dot_nt(q2, k) + bias_ref[variant, :, j].reshape(NA_HEADS_PER_STEP * nq, nk), ks

    ones = jnp.ones((nk, LANES), BF16)

    def softmax(s):
        m = jnp.max(s, axis=-1, keepdims=True)
        return jnp.exp(s - m).astype(BF16)

    def finish(sub, j, p, ks):
        v = v_ref[0, j, pl.ds(ks, nk), :]
        ov = _dot(p, jnp.concatenate([v, ones], axis=1))
        o2 = ov[:, :LANES] / ov[:, LANES:]
        o = jnp.where(first_head, o2[:nq], o2[nq:]).astype(BF16)
        for a in range(NA_QROWS):
            row0 = (sub * NA_QROWS + a) * GRID_W + j * NA_KW
            o_ref[0, row0:row0 + NA_KW, :] = o[a * NA_KW:(a + 1) * NA_KW, :]

    n = len(units)
    sc, pr = {}, {}
    for t in range(n + 2):
        if t < n:
            sc[t] = scores(*units[t])
        if 0 <= t - 1 < n:
            s, ks = sc.pop(t - 1)
            pr[t - 1] = (softmax(s), ks)
        if 0 <= t - 2 < n:
            finish(*units[t - 2], *pr.pop(t - 2))


def _na_call(layer, q_cb, k_band, v_band, bias):
    B, _, tq, _ = q_cb.shape
    T = tq * NA_N_CB
    rows = T // GRID_W
    n_steps = rows // (NA_QROWS * NA_BLOCKS_PER_STEP)
    n_pairs = NA_HEADS // NA_HEADS_PER_STEP
    nq = NA_QROWS * NA_KW
    nk = NA_KROWS * NA_BAND
    kv_spec = pl.BlockSpec((1, NA_N_CB, T // 2, LANES), lambda hp, b, i: (b, 0, 0, hp))
    return pl.pallas_call(
        functools.partial(_na_kernel, rows=rows),
        grid=(n_pairs, B, n_steps),
        in_specs=[
            pl.BlockSpec((1, NA_N_CB, nq * NA_BLOCKS_PER_STEP, LANES), lambda hp, b, i: (b, 0, i, hp)),
            kv_spec, kv_spec,
            pl.BlockSpec((None, 3, NA_HEADS_PER_STEP, NA_N_CB, nq, nk),
                         lambda hp, b, i: (layer, 0, hp, 0, 0, 0), pipeline_mode=pl.Buffered(1)),
        ],
        out_specs=pl.BlockSpec((1, NA_QROWS * NA_BLOCKS_PER_STEP * GRID_W, LANES), lambda hp, b, i: (b, i, hp)),
        out_shape=jax.ShapeDtypeStruct((B, T, NA_W), BF16),
        compiler_params=pltpu.CompilerParams(
            dimension_semantics=("arbitrary", "arbitrary", "arbitrary"), vmem_limit_bytes=VMEM_LIMIT_BYTES),
        name="na_attn",
    )(q_cb, k_band, v_band, bias)


def _lru_gate_weights(wa, wi):
    L = wa.shape[0]
    ng = LRU_BLOCKS // LRU_GROUP

    def blockdiag(w):
        w = w.reshape(L, 2, ng, LRU_GROUP, LRU_BW, LRU_BW)
        eye = jnp.eye(LRU_GROUP, dtype=w.dtype)
        full = w[:, :, :, :, :, None, :] * eye[:, None, :, None]
        return full.reshape(L, 2, ng, LANES, LANES)

    return jnp.concatenate([blockdiag(wa), blockdiag(wi)], axis=-1).astype(BF16)


def _lru_gate_bias(ba, bi):
    L = ba.shape[0]
    ng = LRU_W // LANES
    return jnp.concatenate([ba.reshape(L, 2, ng, 1, LANES), bi.reshape(L, 2, ng, 1, LANES)], axis=-1)


def _softplus(x):
    return jnp.maximum(x, 0.0) + jnp.log1p(jnp.exp(-jnp.abs(x)))


def _lru_inputs(prev_ref, x_ref, next_ref, has_prev, has_next, cw_ref, cb_ref, w_ref, b_ref, lam_ref, d,
                a_scr, u_scr):
    assert CONV_W == 4
    ct, w = x_ref.shape[1], x_ref.shape[2]
    nk = ct // SUBLANES
    x = x_ref[0].reshape(nk, SUBLANES, w)
    sub = lax.broadcasted_iota(jnp.int32, (SUBLANES, w), 0)
    prev = jnp.where(has_prev, prev_ref[0], 0.0)
    nxt = jnp.where(has_next, next_ref[0], 0.0)
    back1 = jnp.where(sub == 0, pltpu.roll(prev[SUBLANES:], 1, 0), pltpu.roll(x[nk - 1], 1, 0))
    back2 = jnp.where(sub == 0, pltpu.roll(prev[:SUBLANES], 1, 0), pltpu.roll(x[nk - 2], 1, 0))
    fwd1 = jnp.where(sub == SUBLANES - 1, pltpu.roll(nxt, SUBLANES - 1, 0), pltpu.roll(x[0], SUBLANES - 1, 0))
    taps = (
        jnp.concatenate([back2[None], back1[None], x[:nk - 2]], axis=0),
        jnp.concatenate([back1[None], x[:nk - 1]], axis=0),
        x,
        jnp.concatenate([x[1:], fwd1[None]], axis=0),
    )
    xc = cb_ref[...]
    for t in range(CONV_W):
        xc = xc + taps[t] * cw_ref[t:t + 1, :]
    xc = xc.reshape(ct, w)
    xcb = xc.astype(BF16)
    sp_c = LRU_C * _softplus(-lam_ref[d:d + 1, :])
    for g in range(LRU_W // LANES):
        sl = slice(g * LANES, (g + 1) * LANES)
        z = _dot(xcb[:, sl], w_ref[d, g]) + b_ref[d, g]
        r = jax.nn.sigmoid(z[:, :LANES])
        gi = jax.nn.sigmoid(z[:, LANES:])
        neg_log_a = r * sp_c[:, sl]
        a = jnp.exp(-neg_log_a)
        y = jnp.tanh(neg_log_a) * (a * a + 1.0)
        u = (y * lax.rsqrt(jnp.maximum(y, F32_TINY))) * (gi * xc[:, sl])
        a_scr[:, sl] = a
        u_scr[:, sl] = u


def _slab_scan(a, b, reverse):
    row = lax.broadcasted_iota(jnp.int32, a.shape, 0)
    for sh in (1, 2, 4):
        if reverse:
            ok = row < SUBLANES - sh
            a_sh = pltpu.roll(a, SUBLANES - sh, 0)
            b_sh = pltpu.roll(b, SUBLANES - sh, 0)
        else:
            ok = row >= sh
            a_sh = pltpu.roll(a, sh, 0)
            b_sh = pltpu.roll(b, sh, 0)
        b = a * jnp.where(ok, b_sh, 0.0) + b
        a = a * jnp.where(ok, a_sh, 1.0)
    return a, b


def _segment_scan(a_scr, u_scr, h_ref, carry_scr, reverse):
    ct, w = a_scr.shape
    nk = ct // SUBLANES

    def group(k):
        kk = (nk - 1 - k) if reverse else k
        return slice(kk * SUBLANES, (kk + 1) * SUBLANES)

    h = jnp.zeros((SUBLANES, w), F32)
    acc = jnp.ones((SUBLANES, w), F32)
    for k in range(nk):
        rows = group(k)
        a = a_scr[rows, :]
        h = a * h + u_scr[rows, :]
        acc = a * acc
        h_ref[0, rows, :] = h
        a_scr[rows, :] = acc
    a_inc, h_inc = _slab_scan(acc, h, reverse)
    state = h_inc + a_inc * carry_scr[...]
    sub = lax.broadcasted_iota(jnp.int32, (SUBLANES, w), 0)
    if reverse:
        carry_out = state[0:1, :]
        incoming = jnp.where(sub == SUBLANES - 1, carry_scr[...], pltpu.roll(state, SUBLANES - 1, 0))
    else:
        carry_out = state[SUBLANES - 1:SUBLANES, :]
        incoming = jnp.where(sub == 0, carry_scr[...], pltpu.roll(state, 1, 0))
    carry_scr[...] = carry_out
    for k in range(nk):
        rows = group(k)
        h_ref[0, rows, :] = h_ref[0, rows, :] + a_scr[rows, :] * incoming


def _lru_kernel(xf_ref, pf_ref, nf_ref, xb_ref, pb_ref, nb_ref, cw_ref, cb_ref, w_ref, b_ref, lam_ref,
                hf_ref, hb_ref, af_scr, uf_scr, ab_scr, ub_scr, cf_scr, cbk_scr):
    c = pl.program_id(1)
    nc = pl.num_programs(1)

    @pl.when(c == 0)
    def _():
        cf_scr[...] = jnp.zeros_like(cf_scr)
        cbk_scr[...] = jnp.zeros_like(cbk_scr)

    _lru_inputs(pf_ref, xf_ref, nf_ref, c > 0, c < nc - 1, cw_ref, cb_ref, w_ref, b_ref, lam_ref, 0,
                af_scr, uf_scr)
    _lru_inputs(pb_ref, xb_ref, nb_ref, c < nc - 1, c > 0, cw_ref, cb_ref, w_ref, b_ref, lam_ref, 1,
                ab_scr, ub_scr)
    _segment_scan(af_scr, uf_scr, hf_ref, cf_scr, False)
    _segment_scan(ab_scr, ub_scr, hb_ref, cbk_scr, True)


def _lru_call(layer, x_lru, conv_w, conv_b, w_gate, b_gate, lam):
    B, T, W = x_lru.shape
    ct = LRU_CHUNK
    nc = T // ct
    tail = 2 * SUBLANES
    n_tail = ct // tail
    n_head = ct // SUBLANES

    main_f = pl.BlockSpec((1, ct, W), lambda b, c: (b, c, 0))
    prev_f = pl.BlockSpec((1, tail, W), lambda b, c: (b, jnp.maximum(c * n_tail - 1, 0), 0))
    next_f = pl.BlockSpec((1, SUBLANES, W), lambda b, c: (b, jnp.minimum(c + 1, nc - 1) * n_head, 0))
    main_b = pl.BlockSpec((1, ct, W), lambda b, c: (b, nc - 1 - c, 0))
    prev_b = pl.BlockSpec((1, tail, W), lambda b, c: (b, jnp.maximum((nc - 1 - c) * n_tail - 1, 0), 0))
    next_b = pl.BlockSpec((1, SUBLANES, W), lambda b, c: (b, jnp.minimum(nc - c, nc - 1) * n_head, 0))
    params = (conv_w, conv_b, w_gate, b_gate, lam)
    return pl.pallas_call(
        _lru_kernel,
        grid=(B, nc),
        in_specs=[main_f, prev_f, next_f, main_b, prev_b, next_b] + [_resident(p.shape[1:], layer) for p in params],
        out_specs=(main_f, main_b),
        out_shape=(jax.ShapeDtypeStruct((B, T, W), F32), jax.ShapeDtypeStruct((B, T, W), F32)),
        scratch_shapes=[pltpu.VMEM((ct, W), F32)] * 4 + [pltpu.VMEM((1, W), F32)] * 2,
        compiler_params=pltpu.CompilerParams(dimension_semantics=("arbitrary", "arbitrary")),
        name="rg_lru",
    )(x_lru, x_lru, x_lru, x_lru, x_lru, x_lru, *params)


def _merge_ffn_kernel(x_ref, yna_ref, hf_ref, hb_ref, glru_ref, qca_ref, kv_ref,
                      gm_ref, wg_ref, bg_ref, wna_ref, wlru_ref, wca_ref, wout_ref, gpost_ref,
                      g1_ref, wup_ref, wdn_ref, g2_ref, *o_refs, split):
    toks = _subtiles(x_ref.shape[1])
    d = x_ref.shape[-1]
    scale = CA_HEAD_DIM ** -0.5
    scores = [[_dot_nt(qca_ref[0, tok, hd * CA_HEAD_DIM:(hd + 1) * CA_HEAD_DIM],
                       kv_ref[0, 0, :, hd * CA_HEAD_DIM:(hd + 1) * CA_HEAD_DIM]) * scale for tok in toks]
              for hd in range(CA_HEADS)]
    x1s = [x_ref[0, tok, :] for tok in toks]
    hs = [_rms(x1, gm_ref[...]).astype(BF16) for x1 in x1s]

    heads = [[] for _ in toks]
    for hd in range(CA_HEADS):
        v = kv_ref[0, 0, :, CA_W + hd * CA_HEAD_DIM:CA_W + (hd + 1) * CA_HEAD_DIM]
        for t in range(len(toks)):
            s = scores[hd][t]
            m = jnp.max(s, axis=-1, keepdims=True)
            p = jnp.exp(s - m)
            l = jnp.sum(p, axis=-1, keepdims=True)
            heads[t].append((_dot(p.astype(BF16), v) / l).astype(BF16))
    y_cas = [jnp.concatenate(hh, axis=-1) for hh in heads]

    def gate(h, n):
        return jax.nn.sigmoid(_dot(h, wg_ref[:, n * d:(n + 1) * d]) + bg_ref[:, n * d:(n + 1) * d])

    merged = [gate(h, 0) * _dot(yna_ref[0, tok, :], wna_ref[...]) for tok, h in zip(toks, hs)]
    h_lru = _segment_major(hf_ref[0] + hb_ref[0])
    merged = [mg + gate(h, 1) * _dot((h_lru[tok, :] * jax.nn.gelu(glru_ref[0, tok, :])).astype(BF16), wlru_ref[...])
              for tok, h, mg in zip(toks, hs, merged)]
    merged = [mg + gate(h, 2) * _dot(y_ca, wca_ref[...]) for h, mg, y_ca in zip(hs, merged, y_cas)]
    x2s = [x1 + _rms(_dot(mg.astype(BF16), wout_ref[...]), gpost_ref[...]) for x1, mg in zip(x1s, merged)]
    outs = _swiglu_half_step(x2s, g1_ref[...], wup_ref, wdn_ref, g2_ref[...])

    def store(o_ref):
        for tok, out in zip(toks, outs):
            o_ref[0, tok, :] = out

    if split is None:
        store(o_refs[0])
    else:
        first = pl.program_id(0) < split
        pl.when(first)(lambda: store(o_refs[0]))
        pl.when(jnp.logical_not(first))(lambda: store(o_refs[1]))


def _merge_ffn_call(layer, x1, y_na, h_f, h_b, g_lru, q_ca, kv, params, out_batches=None):
    B, T, D = x1.shape
    M = kv.shape[2]
    tm = TOKEN_TILE
    tok = lambda w: pl.BlockSpec((1, tm, w), lambda b, i: (b, i, 0))
    in_specs = [tok(D), tok(NA_W), tok(LRU_W), tok(LRU_W), tok(LRU_W), tok(CA_W),
                pl.BlockSpec((1, 1, M, 2 * CA_W), lambda b, i: (layer, b, 0, 0))]
    in_specs += [_resident(p.shape[1:], layer) for p in params]
    if out_batches is None:
        out_specs, out_shape, split = tok(D), jax.ShapeDtypeStruct((B, T, D), F32), None
    else:
        assert sum(out_batches) == B
        out_specs = tuple(_group_specs((1, tm, D), out_batches, T // tm))
        out_shape = tuple(jax.ShapeDtypeStruct((n, T, D), F32) for n in out_batches)
        split = out_batches[0]
    return pl.pallas_call(
        functools.partial(_merge_ffn_kernel, split=split),
        grid=(B, T // tm),
        in_specs=in_specs,
        out_specs=out_specs,
        out_shape=out_shape,
        compiler_params=pltpu.CompilerParams(
            dimension_semantics=("arbitrary", "arbitrary"), vmem_limit_bytes=VMEM_LIMIT_BYTES),
        name="merge_ffn",
    )(x1, y_na, h_f, h_b, g_lru, q_ca, kv, *params)


def kernel(x_prompt, x_sample, mem_prompt, mem_sample, g_ffn1_pre, w_ffn1_up, w_ffn1_down, g_ffn1_post, g_mix_pre, w_in, na_rpb, conv_w, conv_b, lru_wa, lru_ba, lru_wi, lru_bi, lru_lambda, g_mem, w_mem_kv, w_gate, b_gate, w_branch_na, w_branch_lru, w_branch_ca, w_out, g_mix_post, g_ffn2_pre, w_ffn2_up, w_ffn2_down, g_ffn2_post):
    assert x_prompt.shape[1:] == x_sample.shape[1:] and mem_prompt.shape[1:] == mem_sample.shape[1:]
    batches = (x_prompt.shape[0], x_sample.shape[0])
    mem = jnp.concatenate([mem_prompt, mem_sample], axis=0)
    T = x_prompt.shape[1]
    L = w_in.shape[0]
    rows = T // GRID_W
    assert rows % (NA_QROWS * NA_BLOCKS_PER_STEP) == 0 and rows >= NA_KROWS
    assert T % TOKEN_TILE == 0 and LRU_CHUNK == TOKEN_TILE
    assert w_ffn1_down.shape[1] % MXU_DIM == 0 and w_ffn2_down.shape[1] % MXU_DIM == 0

    bf = lambda w: w.astype(BF16)
    vec = lambda g: g.reshape(L, 1, g.shape[-1])
    w1u, w1d, w2u, w2d = bf(w_ffn1_up), bf(w_ffn1_down), bf(w_ffn2_up), bf(w_ffn2_down)
    win, wg, wout, wkv = bf(w_in), bf(w_gate), bf(w_out), bf(w_mem_kv)
    wna, wlru, wca = bf(w_branch_na), bf(w_branch_lru), bf(w_branch_ca)
    lru_w = _lru_gate_weights(lru_wa, lru_wi)
    lru_b = _lru_gate_bias(lru_ba, lru_bi)
    bias = _na_bias_tables(na_rpb, rows)
    kv = _mem_kv_call(mem, vec(g_mem), wkv)

    xs = (x_prompt, x_sample)
    for l in range(L):
        x1, q_cb, k_band, v_band, x_lru, g_lru, q_ca = _ffn_proj_call(
            l, xs, vec(g_ffn1_pre), w1u, w1d, vec(g_ffn1_post), vec(g_mix_pre), win)
        y_na = _na_call(l, q_cb, k_band, v_band, bias)
        h_f, h_b = _lru_call(l, x_lru, conv_w, vec(conv_b), lru_w, lru_b, lru_lambda)
        out = _merge_ffn_call(
            l, x1, y_na, h_f, h_b, g_lru, q_ca, kv,
            (vec(g_mix_pre), wg, vec(b_gate), wna, wlru, wca, wout, vec(g_mix_post),
             vec(g_ffn2_pre), w2u, w2d, vec(g_ffn2_post)),
            out_batches=batches if l == L - 1 else None)
        xs = (out,)
    return tuple(out)
```

```python
import functools

import numpy as np
import jax
import jax.numpy as jnp
from jax import lax
from jax.experimental import pallas as pl
from jax.experimental.pallas import tpu as pltpu

GRID_W = 64
NA_HEADS = 8
NA_HEAD_DIM = 64
NA_W = NA_HEADS * NA_HEAD_DIM
NA_KH = 8
NA_KW = 16
NA_N_CB = GRID_W // NA_KW
NA_BAND = 2 * NA_KW
LRU_W = 512
LRU_BLOCKS = 8
LRU_BW = LRU_W // LRU_BLOCKS
CONV_W = 4
LRU_C = 8.0
CA_HEADS = 4
CA_HEAD_DIM = 128
CA_W = CA_HEADS * CA_HEAD_DIM
EPS = 1e-6
NEG = -1e30
F32_TINY = float(np.finfo(np.float32).tiny)

LANES = 128
SUBLANES = 8
MXU_DIM = 256
VMEM_BYTES = 64 * 1024 * 1024
VMEM_LIMIT_BYTES = VMEM_BYTES - 4 * 1024 * 1024

TOKEN_TILE = 512
SUBTILES = 2
FF_CHUNK_TILES = 6
NA_QROWS = 8
NA_KROWS = 2 * NA_KH
NA_HEADS_PER_STEP = LANES // NA_HEAD_DIM
NA_BLOCKS_PER_STEP = 8
LRU_CHUNK = 512
LRU_GROUP = LANES // LRU_BW

BF16 = jnp.bfloat16
F32 = jnp.float32


def _rms(x, g):
    return x * lax.rsqrt(jnp.mean(x * x, axis=-1, keepdims=True) + EPS) * g


def _dot(a, b):
    return jnp.dot(a, b, preferred_element_type=F32)


def _dot_nt(a, b):
    return lax.dot_general(a, b, (((1,), (1,)), ((), ())), preferred_element_type=F32)


def _ff_chunks(d_ff):
    tiles = d_ff // MXU_DIM
    widths = []
    while tiles > 0:
        n = min(FF_CHUNK_TILES, tiles)
        widths.append(n * MXU_DIM)
        tiles -= n
    return widths


def _swiglu_half_step(xs, g_pre, w_up_ref, w_down_ref, g_post):
    d_ff = w_down_ref.shape[0]
    hs = [_rms(x, g_pre).astype(BF16) for x in xs]
    accs = [None] * len(xs)
    lo = 0
    for cw in _ff_chunks(d_ff):
        for t, h in enumerate(hs):
            a = _dot(h, w_up_ref[:, lo:lo + cw])
            b = _dot(h, w_up_ref[:, d_ff + lo:d_ff + lo + cw])
            act = (a * jax.nn.sigmoid(a) * b).astype(BF16)
            part = _dot(act, w_down_ref[lo:lo + cw, :])
            accs[t] = part if accs[t] is None else accs[t] + part
        lo += cw
    return [x + 0.5 * _rms(acc, g_post) for x, acc in zip(xs, accs)]


def _subtiles(n_tokens):
    tm = n_tokens // SUBTILES
    return [slice(t * tm, (t + 1) * tm) for t in range(SUBTILES)]


def _time_major(x):
    n, w = x.shape
    return jnp.swapaxes(x.reshape(SUBLANES, n // SUBLANES, w), 0, 1).reshape(n, w)


def _segment_major(x):
    n, w = x.shape
    return jnp.swapaxes(x.reshape(n // SUBLANES, SUBLANES, w), 0, 1).reshape(n, w)


def _band_start(j):
    return int(np.clip(j * NA_KW - NA_KW // 2, 0, GRID_W - NA_BAND))


def _ffn_proj_kernel(*refs, split):
    n_in = 1 if split is None else 2
    g1_ref, wup_ref, wdn_ref, g2_ref, gm_ref, win_ref = refs[n_in:n_in + 6]
    x1_ref, qcb_ref, kband_ref, vband_ref, xlru_ref, glru_ref, qca_ref = refs[n_in + 6:]
    toks = _subtiles(x1_ref.shape[1])
    if split is None:
        xs = [refs[0][0, tok, :] for tok in toks]
    else:
        first = pl.program_id(0) < split
        xs = [jnp.where(first, refs[0][0, tok, :], refs[1][0, tok, :]) for tok in toks]
    x1s = _swiglu_half_step(xs, g1_ref[...], wup_ref, wdn_ref, g2_ref[...])
    for tok, x1 in zip(toks, x1s):
        x1_ref[0, tok, :] = x1
    hs = [_rms(x1, gm_ref[...]).astype(BF16) for x1 in x1s]
    o = 3 * NA_W
    xlru_ref[0] = _time_major(jnp.concatenate([_dot(h, win_ref[:, o:o + LRU_W]) for h in hs], axis=0))
    for tok, h in zip(toks, hs):
        glru_ref[0, tok, :] = _dot(h, win_ref[:, o + LRU_W:o + 2 * LRU_W])
    for t, (tok, h) in enumerate(zip(toks, hs)):
        qkv = _dot(h, win_ref[:, 0:o])
        rows = qkv.shape[0] // GRID_W
        q = (qkv[:, 0:NA_W] * (NA_HEAD_DIM ** -0.5)).reshape(rows, GRID_W, NA_W)
        k = qkv[:, NA_W:2 * NA_W].reshape(rows, GRID_W, NA_W)
        v = qkv[:, 2 * NA_W:3 * NA_W].reshape(rows, GRID_W, NA_W)
        for j in range(NA_N_CB):
            bs = _band_start(j)
            qs = slice(t * rows * NA_KW, (t + 1) * rows * NA_KW)
            ks = slice(t * rows * NA_BAND, (t + 1) * rows * NA_BAND)
            qcb_ref[0, j, qs, :] = q[:, j * NA_KW:(j + 1) * NA_KW, :].reshape(rows * NA_KW, NA_W).astype(BF16)
            kband_ref[0, j, ks, :] = k[:, bs:bs + NA_BAND, :].reshape(rows * NA_BAND, NA_W).astype(BF16)
            vband_ref[0, j, ks, :] = v[:, bs:bs + NA_BAND, :].reshape(rows * NA_BAND, NA_W).astype(BF16)
        qca_ref[0, tok, :] = _dot(h, win_ref[:, o + 2 * LRU_W:o + 2 * LRU_W + CA_W]).astype(BF16)


def _resident(shape, layer):
    nd = len(shape)
    return pl.BlockSpec((None,) + tuple(shape), lambda *_: (layer,) + (0,) * nd,
                        pipeline_mode=pl.Buffered(1))


def _group_specs(block, batches, n_tiles):
    b0 = batches[0]
    return [
        pl.BlockSpec(block, lambda b, i: (jnp.minimum(b, b0 - 1), jnp.where(b < b0, i, n_tiles - 1), 0)),
        pl.BlockSpec(block, lambda b, i: (jnp.maximum(b - b0, 0), jnp.where(b >= b0, i, 0), 0)),
    ]


def _ffn_proj_call(layer, xs, g1, wup, wdn, g2, gm, win):
    T, D = xs[0].shape[1:]
    B = sum(x.shape[0] for x in xs)
    tm = TOKEN_TILE
    tok = lambda w: pl.BlockSpec((1, tm, w), lambda b, i: (b, i, 0))
    if len(xs) == 1:
        x_specs, split = [tok(D)], None
    else:
        x_specs, split = _group_specs((1, tm, D), [x.shape[0] for x in xs], T // tm), xs[0].shape[0]
    out_shape = (
        jax.ShapeDtypeStruct((B, T, D), F32),
        jax.ShapeDtypeStruct((B, NA_N_CB, T // NA_N_CB, NA_W), BF16),
        jax.ShapeDtypeStruct((B, NA_N_CB, T // 2, NA_W), BF16),
        jax.ShapeDtypeStruct((B, NA_N_CB, T // 2, NA_W), BF16),
        jax.ShapeDtypeStruct((B, T, LRU_W), F32),
        jax.ShapeDtypeStruct((B, T, LRU_W), F32),
        jax.ShapeDtypeStruct((B, T, CA_W), BF16),
    )
    out_specs = (
        tok(D),
        pl.BlockSpec((1, NA_N_CB, tm // NA_N_CB, NA_W), lambda b, i: (b, 0, i, 0)),
        pl.BlockSpec((1, NA_N_CB, tm // 2, NA_W), lambda b, i: (b, 0, i, 0)),
        pl.BlockSpec((1, NA_N_CB, tm // 2, NA_W), lambda b, i: (b, 0, i, 0)),
        tok(LRU_W), tok(LRU_W), tok(CA_W),
    )
    in_specs = x_specs + [_resident(w.shape[1:], layer) for w in (g1, wup, wdn, g2, gm, win)]
    return pl.pallas_call(
        functools.partial(_ffn_proj_kernel, split=split),
        grid=(B, T // tm),
        in_specs=in_specs,
        out_specs=out_specs,
        out_shape=out_shape,
        compiler_params=pltpu.CompilerParams(
            dimension_semantics=("arbitrary", "arbitrary"), vmem_limit_bytes=VMEM_LIMIT_BYTES),
        name="ffn_proj",
    )(*xs, g1, wup, wdn, g2, gm, win)


def _mem_kv_kernel(mem_ref, g_ref, w_ref, kv_ref):
    h = _rms(mem_ref[0], g_ref[...]).astype(BF16)
    kv_ref[0, 0] = _dot(h, w_ref[...]).astype(BF16)


def _mem_kv_call(mem, g_mem, w_kv):
    B, M, D = mem.shape
    L = w_kv.shape[0]
    return pl.pallas_call(
        _mem_kv_kernel,
        grid=(L, B),
        in_specs=[
            pl.BlockSpec((1, M, D), lambda l, b: (b, 0, 0)),
            pl.BlockSpec((None, 1, D), lambda l, b: (l, 0, 0)),
            pl.BlockSpec((None, D, 2 * CA_W), lambda l, b: (l, 0, 0)),
        ],
        out_specs=pl.BlockSpec((1, 1, M, 2 * CA_W), lambda l, b: (l, b, 0, 0)),
        out_shape=jax.ShapeDtypeStruct((L, B, M, 2 * CA_W), BF16),
        compiler_params=pltpu.CompilerParams(dimension_semantics=("arbitrary", "arbitrary")),
        name="mem_kv",
    )(mem, g_mem, w_kv)


def _na_key_row_start(i, rows):
    return jnp.clip(i * NA_QROWS - NA_KH // 2, 0, rows - NA_KROWS)


def _na_bias_tables(rpb, rows):
    L, H = rpb.shape[:2]
    n_blocks = rows // NA_QROWS
    n_dr = 2 * NA_KH - 1
    j = np.arange(NA_N_CB)[:, None, None]
    qc = j * NA_KW + np.arange(NA_KW)[None, :, None]
    bs = np.clip(j * NA_KW - NA_KW // 2, 0, GRID_W - NA_BAND)
    kc = bs + np.arange(NA_BAND)[None, None, :]
    ws = np.clip(qc - NA_KW // 2, 0, GRID_W - NA_KW)
    col_ok = (kc >= ws) & (kc < ws + NA_KW)
    rp = jnp.pad(rpb, ((0, 0), (0, 0), (0, 0), (NA_BAND, NA_BAND)))
    cols = []
    for jj in range(NA_N_CB):
        for q in range(NA_KW):
            shift = NA_BAND + int(bs[jj, 0, 0]) - (jj * NA_KW + q) + (NA_KW - 1)
            cols.append(rp[..., shift:shift + NA_BAND])
    t1 = jnp.stack(cols).reshape(NA_N_CB, NA_KW, L, H, n_dr, NA_BAND)
    t1 = jnp.where(jnp.asarray(col_ok)[:, :, None, None, None, :], t1, NEG)
    t1 = t1.reshape(NA_N_CB, NA_KW, L, H, n_dr * NA_BAND)
    runs = []
    for i in (0, min(1, n_blocks - 1), n_blocks - 1):
        ks = int(np.clip(i * NA_QROWS - NA_KH // 2, 0, rows - NA_KROWS))
        for aa in range(NA_QROWS):
            r = i * NA_QROWS + aa
            rs = int(np.clip(r - NA_KH // 2, 0, rows - NA_KH))
            lo = rs - ks
            dr0 = rs - r + NA_KH - 1
            assert 0 <= lo <= NA_KROWS - NA_KH and 0 <= dr0 <= n_dr - NA_KH
            run = t1[..., dr0 * NA_BAND:(dr0 + NA_KH) * NA_BAND]
            runs.append(jnp.pad(run, ((0, 0),) * 4 + ((lo * NA_BAND, (NA_KROWS - NA_KH - lo) * NA_BAND),),
                                constant_values=NEG))
    nk = NA_KROWS * NA_BAND
    bias = jnp.stack(runs).reshape(3, NA_QROWS, NA_N_CB, NA_KW, L, H, nk)
    bias = jnp.transpose(bias, (4, 0, 5, 2, 1, 3, 6))
    return bias.reshape(L, 3, H, NA_N_CB, NA_QROWS * NA_KW, nk)


def _na_kernel(q_ref, k_ref, v_ref, bias_ref, o_ref, *, rows):
    step = pl.program_id(2)
    n_blocks = rows // NA_QROWS
    nq = NA_QROWS * NA_KW
    nk = NA_KROWS * NA_BAND
    lane = lax.broadcasted_iota(jnp.int32, (nq, LANES), 1)
    first_head = lane < NA_HEAD_DIM
    units = [(sub, j) for sub in range(NA_BLOCKS_PER_STEP) for j in range(NA_N_CB)]

    def scores(sub, j):
        i = step * NA_BLOCKS_PER_STEP + sub
        ks = pl.multiple_of(_na_key_row_start(i, rows) * NA_BAND, NA_BAND * (NA_KH // 2))
        variant = jnp.where(i == 0, 0, jnp.where(i == n_blocks - 1, 2, 1))
        q = q_ref[0, j, sub * nq:(sub + 1) * nq, :]
        k = k_ref[0, j, pl.ds(ks, nk), :]
        zero = jnp.zeros_like(q)
        q2 = jnp.concatenate([jnp.where(first_head, q, zero), jnp.where(first_head, zero, q)], axis=0)
        return _dot_nt(q2, k) + bias_ref[variant, :, j].reshape(NA_HEADS_PER_STEP * nq, nk), ks

    ones = jnp.ones((nk, LANES), BF16)

    def softmax(s):
        m = jnp.max(s, axis=-1, keepdims=True)
        return jnp.exp(s - m).astype(BF16)

    def finish(sub, j, p, ks):
        v = v_ref[0, j, pl.ds(ks, nk), :]
        ov = _dot(p, jnp.concatenate([v, ones], axis=1))
        o2 = ov[:, :LANES] / ov[:, LANES:]
        o = jnp.where(first_head, o2[:nq], o2[nq:]).astype(BF16)
        for a in range(NA_QROWS):
            row0 = (sub * NA_QROWS + a) * GRID_W + j * NA_KW
            o_ref[0, row0:row0 + NA_KW, :] = o[a * NA_KW:(a + 1) * NA_KW, :]

    n = len(units)
    sc, pr = {}, {}
    for t in range(n + 2):
        if t < n:
            sc[t] = scores(*units[t])
        if 0 <= t - 1 < n:
            s, ks = sc.pop(t - 1)
            pr[t - 1] = (softmax(s), ks)
        if 0 <= t - 2 < n:
            finish(*units[t - 2], *pr.pop(t - 2))


def _na_call(layer, q_cb, k_band, v_band, bias):
    B, _, tq, _ = q_cb.shape
    T = tq * NA_N_CB
    rows = T // GRID_W
    n_steps = rows // (NA_QROWS * NA_BLOCKS_PER_STEP)
    n_pairs = NA_HEADS // NA_HEADS_PER_STEP
    nq = NA_QROWS * NA_KW
    nk = NA_KROWS * NA_BAND
    kv_spec = pl.BlockSpec((1, NA_N_CB, T // 2, LANES), lambda hp, b, i: (b, 0, 0, hp))
    return pl.pallas_call(
        functools.partial(_na_kernel, rows=rows),
        grid=(n_pairs, B, n_steps),
        in_specs=[
            pl.BlockSpec((1, NA_N_CB, nq * NA_BLOCKS_PER_STEP, LANES), lambda hp, b, i: (b, 0, i, hp)),
            kv_spec, kv_spec,
            pl.BlockSpec((None, 3, NA_HEADS_PER_STEP, NA_N_CB, nq, nk),
                         lambda hp, b, i: (layer, 0, hp, 0, 0, 0), pipeline_mode=pl.Buffered(1)),
        ],
        out_specs=pl.BlockSpec((1, NA_QROWS * NA_BLOCKS_PER_STEP * GRID_W, LANES), lambda hp, b, i: (b, i, hp)),
        out_shape=jax.ShapeDtypeStruct((B, T, NA_W), BF16),
        compiler_params=pltpu.CompilerParams(
            dimension_semantics=("arbitrary", "arbitrary", "arbitrary"), vmem_limit_bytes=VMEM_LIMIT_BYTES),
        name="na_attn",
    )(q_cb, k_band, v_band, bias)


def _lru_gate_weights(wa, wi):
    L = wa.shape[0]
    ng = LRU_BLOCKS // LRU_GROUP

    def blockdiag(w):
        w = w.reshape(L, 2, ng, LRU_GROUP, LRU_BW, LRU_BW)
        eye = jnp.eye(LRU_GROUP, dtype=w.dtype)
        full = w[:, :, :, :, :, None, :] * eye[:, None, :, None]
        return full.reshape(L, 2, ng, LANES, LANES)

    return jnp.concatenate([blockdiag(wa), blockdiag(wi)], axis=-1).astype(BF16)


def _lru_gate_bias(ba, bi):
    L = ba.shape[0]
    ng = LRU_W // LANES
    return jnp.concatenate([ba.reshape(L, 2, ng, 1, LANES), bi.reshape(L, 2, ng, 1, LANES)], axis=-1)


def _softplus(x):
    return jnp.maximum(x, 0.0) + jnp.log1p(jnp.exp(-jnp.abs(x)))


def _lru_inputs(prev_ref, x_ref, next_ref, has_prev, has_next, cw_ref, cb_ref, w_ref, b_ref, lam_ref, d,
                a_scr, u_scr):
    assert CONV_W == 4
    ct, w = x_ref.shape[1], x_ref.shape[2]
    nk = ct // SUBLANES
    x = x_ref[0].reshape(nk, SUBLANES, w)
    sub = lax.broadcasted_iota(jnp.int32, (SUBLANES, w), 0)
    prev = jnp.where(has_prev, prev_ref[0], 0.0)
    nxt = jnp.where(has_next, next_ref[0], 0.0)
    back1 = jnp.where(sub == 0, pltpu.roll(prev[SUBLANES:], 1, 0), pltpu.roll(x[nk - 1], 1, 0))
    back2 = jnp.where(sub == 0, pltpu.roll(prev[:SUBLANES], 1, 0), pltpu.roll(x[nk - 2], 1, 0))
    fwd1 = jnp.where(sub == SUBLANES - 1, pltpu.roll(nxt, SUBLANES - 1, 0), pltpu.roll(x[0], SUBLANES - 1, 0))
    taps = (
        jnp.concatenate([back2[None], back1[None], x[:nk - 2]], axis=0),
        jnp.concatenate([back1[None], x[:nk - 1]], axis=0),
        x,
        jnp.concatenate([x[1:], fwd1[None]], axis=0),
    )
    xc = cb_ref[...]
    for t in range(CONV_W):
        xc = xc + taps[t] * cw_ref[t:t + 1, :]
    xc = xc.reshape(ct, w)
    xcb = xc.astype(BF16)
    sp_c = LRU_C * _softplus(-lam_ref[d:d + 1, :])
    for g in range(LRU_W // LANES):
        sl = slice(g * LANES, (g + 1) * LANES)
        z = _dot(xcb[:, sl], w_ref[d, g]) + b_ref[d, g]
        r = jax.nn.sigmoid(z[:, :LANES])
        gi = jax.nn.sigmoid(z[:, LANES:])
        neg_log_a = r * sp_c[:, sl]
        a = jnp.exp(-neg_log_a)
        y = jnp.tanh(neg_log_a) * (a * a + 1.0)
        u = (y * lax.rsqrt(jnp.maximum(y, F32_TINY))) * (gi * xc[:, sl])
        a_scr[:, sl] = a
        u_scr[:, sl] = u


def _slab_scan(a, b, reverse):
    row = lax.broadcasted_iota(jnp.int32, a.shape, 0)
    for sh in (1, 2, 4):
        if reverse:
            ok = row < SUBLANES - sh
            a_sh = pltpu.roll(a, SUBLANES - sh, 0)
            b_sh = pltpu.roll(b, SUBLANES - sh, 0)
        else:
            ok = row >= sh
            a_sh = pltpu.roll(a, sh, 0)
            b_sh = pltpu.roll(b, sh, 0)
        b = a * jnp.where(ok, b_sh, 0.0) + b
        a = a * jnp.where(ok, a_sh, 1.0)
    return a, b


def _segment_scan(a_scr, u_scr, h_ref, carry_scr, reverse):
    ct, w = a_scr.shape
    nk = ct // SUBLANES

    def group(k):
        kk = (nk - 1 - k) if reverse else k
        return slice(kk * SUBLANES, (kk + 1) * SUBLANES)

    h = jnp.zeros((SUBLANES, w), F32)
    acc = jnp.ones((SUBLANES, w), F32)
    for k in range(nk):
        rows = group(k)
        a = a_scr[rows, :]
        h = a * h + u_scr[rows, :]
        acc = a * acc
        h_ref[0, rows, :] = h
        a_scr[rows, :] = acc
    a_inc, h_inc = _slab_scan(acc, h, reverse)
    state = h_inc + a_inc * carry_scr[...]
    sub = lax.broadcasted_iota(jnp.int32, (SUBLANES, w), 0)
    if reverse:
        carry_out = state[0:1, :]
        incoming = jnp.where(sub == SUBLANES - 1, carry_scr[...], pltpu.roll(state, SUBLANES - 1, 0))
    else:
        carry_out = state[SUBLANES - 1:SUBLANES, :]
        incoming = jnp.where(sub == 0, carry_scr[...], pltpu.roll(state, 1, 0))
    carry_scr[...] = carry_out
    for k in range(nk):
        rows = group(k)
        h_ref[0, rows, :] = h_ref[0, rows, :] + a_scr[rows, :] * incoming


def _lru_kernel(xf_ref, pf_ref, nf_ref, xb_ref, pb_ref, nb_ref, cw_ref, cb_ref, w_ref, b_ref, lam_ref,
                hf_ref, hb_ref, af_scr, uf_scr, ab_scr, ub_scr, cf_scr, cbk_scr):
    c = pl.program_id(1)
    nc = pl.num_programs(1)

    @pl.when(c == 0)
    def _():
        cf_scr[...] = jnp.zeros_like(cf_scr)
        cbk_scr[...] = jnp.zeros_like(cbk_scr)

    _lru_inputs(pf_ref, xf_ref, nf_ref, c > 0, c < nc - 1, cw_ref, cb_ref, w_ref, b_ref, lam_ref, 0,
                af_scr, uf_scr)
    _lru_inputs(pb_ref, xb_ref, nb_ref, c < nc - 1, c > 0, cw_ref, cb_ref, w_ref, b_ref, lam_ref, 1,
                ab_scr, ub_scr)
    _segment_scan(af_scr, uf_scr, hf_ref, cf_scr, False)
    _segment_scan(ab_scr, ub_scr, hb_ref, cbk_scr, True)


def _lru_call(layer, x_lru, conv_w, conv_b, w_gate, b_gate, lam):
    B, T, W = x_lru.shape
    ct = LRU_CHUNK
    nc = T // ct
    tail = 2 * SUBLANES
    n_tail = ct // tail
    n_head = ct // SUBLANES

    main_f = pl.BlockSpec((1, ct, W), lambda b, c: (b, c, 0))
    prev_f = pl.BlockSpec((1, tail, W), lambda b, c: (b, jnp.maximum(c * n_tail - 1, 0), 0))
    next_f = pl.BlockSpec((1, SUBLANES, W), lambda b, c: (b, jnp.minimum(c + 1, nc - 1) * n_head, 0))
    main_b = pl.BlockSpec((1, ct, W), lambda b, c: (b, nc - 1 - c, 0))
    prev_b = pl.BlockSpec((1, tail, W), lambda b, c: (b, jnp.maximum((nc - 1 - c) * n_tail - 1, 0), 0))
    next_b = pl.BlockSpec((1, SUBLANES, W), lambda b, c: (b, jnp.minimum(nc - c, nc - 1) * n_head, 0))
    params = (conv_w, conv_b, w_gate, b_gate, lam)
    return pl.pallas_call(
        _lru_kernel,
        grid=(B, nc),
        in_specs=[main_f, prev_f, next_f, main_b, prev_b, next_b] + [_resident(p.shape[1:], layer) for p in params],
        out_specs=(main_f, main_b),
        out_shape=(jax.ShapeDtypeStruct((B, T, W), F32), jax.ShapeDtypeStruct((B, T, W), F32)),
        scratch_shapes=[pltpu.VMEM((ct, W), F32)] * 4 + [pltpu.VMEM((1, W), F32)] * 2,
        compiler_params=pltpu.CompilerParams(dimension_semantics=("arbitrary", "arbitrary")),
        name="rg_lru",
    )(x_lru, x_lru, x_lru, x_lru, x_lru, x_lru, *params)


def _merge_ffn_kernel(x_ref, yna_ref, hf_ref, hb_ref, glru_ref, qca_ref, kv_ref,
                      gm_ref, wg_ref, bg_ref, wna_ref, wlru_ref, wca_ref, wout_ref, gpost_ref,
                      g1_ref, wup_ref, wdn_ref, g2_ref, *o_refs, split):
    toks = _subtiles(x_ref.shape[1])
    d = x_ref.shape[-1]
    scale = CA_HEAD_DIM ** -0.5
    scores = [[_dot_nt(qca_ref[0, tok, hd * CA_HEAD_DIM:(hd + 1) * CA_HEAD_DIM],
                       kv_ref[0, 0, :, hd * CA_HEAD_DIM:(hd + 1) * CA_HEAD_DIM]) * scale for tok in toks]
              for hd in range(CA_HEADS)]
    x1s = [x_ref[0, tok, :] for tok in toks]
    hs = [_rms(x1, gm_ref[...]).astype(BF16) for x1 in x1s]

    heads = [[] for _ in toks]
    ca_ones = jnp.ones((kv_ref.shape[2], CA_HEAD_DIM), BF16)
    for hd in range(CA_HEADS):
        v = kv_ref[0, 0, :, CA_W + hd * CA_HEAD_DIM:CA_W + (hd + 1) * CA_HEAD_DIM]
        for t in range(len(toks)):
            s = scores[hd][t]
            m = jnp.max(s, axis=-1, keepdims=True)
            p = jnp.exp(s - m).astype(BF16)
            ov = _dot(p, jnp.concatenate([v, ca_ones], axis=1))
            heads[t].append((ov[:, :CA_HEAD_DIM] / ov[:, CA_HEAD_DIM:]).astype(BF16))
    y_cas = [jnp.concatenate(hh, axis=-1) for hh in heads]

    def gate(h, n):
        return jax.nn.sigmoid(_dot(h, wg_ref[:, n * d:(n + 1) * d]) + bg_ref[:, n * d:(n + 1) * d])

    merged = [gate(h, 0) * _dot(yna_ref[0, tok, :], wna_ref[...]) for tok, h in zip(toks, hs)]
    h_lru = _segment_major(hf_ref[0] + hb_ref[0])
    merged = [mg + gate(h, 1) * _dot((h_lru[tok, :] * jax.nn.gelu(glru_ref[0, tok, :])).astype(BF16), wlru_ref[...])
              for tok, h, mg in zip(toks, hs, merged)]
    merged = [mg + gate(h, 2) * _dot(y_ca, wca_ref[...]) for h, mg, y_ca in zip(hs, merged, y_cas)]
    x2s = [x1 + _rms(_dot(mg.astype(BF16), wout_ref[...]), gpost_ref[...]) for x1, mg in zip(x1s, merged)]
    outs = _swiglu_half_step(x2s, g1_ref[...], wup_ref, wdn_ref, g2_ref[...])

    def store(o_ref):
        for tok, out in zip(toks, outs):
            o_ref[0, tok, :] = out

    if split is None:
        store(o_refs[0])
    else:
        first = pl.program_id(0) < split
        pl.when(first)(lambda: store(o_refs[0]))
        pl.when(jnp.logical_not(first))(lambda: store(o_refs[1]))


def _merge_ffn_call(layer, x1, y_na, h_f, h_b, g_lru, q_ca, kv, params, out_batches=None):
    B, T, D = x1.shape
    M = kv.shape[2]
    tm = TOKEN_TILE
    tok = lambda w: pl.BlockSpec((1, tm, w), lambda b, i: (b, i, 0))
    in_specs = [tok(D), tok(NA_W), tok(LRU_W), tok(LRU_W), tok(LRU_W), tok(CA_W),
                pl.BlockSpec((1, 1, M, 2 * CA_W), lambda b, i: (layer, b, 0, 0))]
    in_specs += [_resident(p.shape[1:], layer) for p in params]
    if out_batches is None:
        out_specs, out_shape, split = tok(D), jax.ShapeDtypeStruct((B, T, D), F32), None
    else:
        assert sum(out_batches) == B
        out_specs = tuple(_group_specs((1, tm, D), out_batches, T // tm))
        out_shape = tuple(jax.ShapeDtypeStruct((n, T, D), F32) for n in out_batches)
        split = out_batches[0]
    return pl.pallas_call(
        functools.partial(_merge_ffn_kernel, split=split),
        grid=(B, T // tm),
        in_specs=in_specs,
        out_specs=out_specs,
        out_shape=out_shape,
        compiler_params=pltpu.CompilerParams(
            dimension_semantics=("arbitrary", "arbitrary"), vmem_limit_bytes=VMEM_LIMIT_BYTES),
        name="merge_ffn",
    )(x1, y_na, h_f, h_b, g_lru, q_ca, kv, *params)


def kernel(x_prompt, x_sample, mem_prompt, mem_sample, g_ffn1_pre, w_ffn1_up, w_ffn1_down, g_ffn1_post, g_mix_pre, w_in, na_rpb, conv_w, conv_b, lru_wa, lru_ba, lru_wi, lru_bi, lru_lambda, g_mem, w_mem_kv, w_gate, b_gate, w_branch_na, w_branch_lru, w_branch_ca, w_out, g_mix_post, g_ffn2_pre, w_ffn2_up, w_ffn2_down, g_ffn2_post):
    assert x_prompt.shape[1:] == x_sample.shape[1:] and mem_prompt.shape[1:] == mem_sample.shape[1:]
    batches = (x_prompt.shape[0], x_sample.shape[0])
    mem = jnp.concatenate([mem_prompt, mem_sample], axis=0)
    T = x_prompt.shape[1]
    L = w_in.shape[0]
    rows = T // GRID_W
    assert rows % (NA_QROWS * NA_BLOCKS_PER_STEP) == 0 and rows >= NA_KROWS
    assert T % TOKEN_TILE == 0 and LRU_CHUNK == TOKEN_TILE
    assert w_ffn1_down.shape[1] % MXU_DIM == 0 and w_ffn2_down.shape[1] % MXU_DIM == 0

    bf = lambda w: w.astype(BF16)
    vec = lambda g: g.reshape(L, 1, g.shape[-1])
    w1u, w1d, w2u, w2d = bf(w_ffn1_up), bf(w_ffn1_down), bf(w_ffn2_up), bf(w_ffn2_down)
    win, wg, wout, wkv = bf(w_in), bf(w_gate), bf(w_out), bf(w_mem_kv)
    wna, wlru, wca = bf(w_branch_na), bf(w_branch_lru), bf(w_branch_ca)
    lru_w = _lru_gate_weights(lru_wa, lru_wi)
    lru_b = _lru_gate_bias(lru_ba, lru_bi)
    bias = _na_bias_tables(na_rpb, rows)
    kv = _mem_kv_call(mem, vec(g_mem), wkv)

    xs = (x_prompt, x_sample)
    for l in range(L):
        x1, q_cb, k_band, v_band, x_lru, g_lru, q_ca = _ffn_proj_call(
            l, xs, vec(g_ffn1_pre), w1u, w1d, vec(g_ffn1_post), vec(g_mix_pre), win)
        y_na = _na_call(l, q_cb, k_band, v_band, bias)
        h_f, h_b = _lru_call(l, x_lru, conv_w, vec(conv_b), lru_w, lru_b, lru_lambda)
        out = _merge_ffn_call(
            l, x1, y_na, h_f, h_b, g_lru, q_ca, kv,
            (vec(g_mix_pre), wg, vec(b_gate), wna, wlru, wca, wout, vec(g_mix_post),
             vec(g_ffn2_pre), w2u, w2d, vec(g_ffn2_post)),
            out_batches=batches if l == L - 1 else None)
        xs = (out,)
    return tuple(out)
```

```python
import functools

import numpy as np
import jax
import jax.numpy as jnp
from jax import lax
from jax.experimental import pallas as pl
from jax.experimental.pallas import tpu as pltpu

GRID_W = 64
NA_HEADS = 8
NA_HEAD_DIM = 64
NA_W = NA_HEADS * NA_HEAD_DIM
NA_KH = 8
NA_KW = 16
NA_N_CB = GRID_W // NA_KW
NA_BAND = 2 * NA_KW
LRU_W = 512
LRU_BLOCKS = 8
LRU_BW = LRU_W // LRU_BLOCKS
CONV_W = 4
LRU_C = 8.0
CA_HEADS = 4
CA_HEAD_DIM = 128
CA_W = CA_HEADS * CA_HEAD_DIM
EPS = 1e-6
NEG = -1e30
F32_TINY = float(np.finfo(np.float32).tiny)

LANES = 128
SUBLANES = 8
MXU_DIM = 256
VMEM_BYTES = 64 * 1024 * 1024
VMEM_LIMIT_BYTES = VMEM_BYTES - 4 * 1024 * 1024

TOKEN_TILE = 512
SUBTILES = 2
FF_CHUNK_TILES = 6
NA_QROWS = 8
NA_KROWS = 2 * NA_KH
NA_HEADS_PER_STEP = LANES // NA_HEAD_DIM
NA_BLOCKS_PER_STEP = 16
LRU_CHUNK = 512
LRU_GROUP = LANES // LRU_BW

BF16 = jnp.bfloat16
F32 = jnp.float32


def _rms(x, g):
    return x * lax.rsqrt(jnp.mean(x * x, axis=-1, keepdims=True) + EPS) * g


def _dot(a, b):
    return jnp.dot(a, b, preferred_element_type=F32)


def _dot_nt(a, b):
    return lax.dot_general(a, b, (((1,), (1,)), ((), ())), preferred_element_type=F32)


def _ff_chunks(d_ff):
    tiles = d_ff // MXU_DIM
    widths = []
    while tiles > 0:
        n = min(FF_CHUNK_TILES, tiles)
        widths.append(n * MXU_DIM)
        tiles -= n
    return widths


def _swiglu_half_step(xs, g_pre, w_up_ref, w_down_ref, g_post):
    d_ff = w_down_ref.shape[0]
    hs = [_rms(x, g_pre).astype(BF16) for x in xs]
    accs = [None] * len(xs)
    lo = 0
    for cw in _ff_chunks(d_ff):
        for t, h in enumerate(hs):
            a = _dot(h, w_up_ref[:, lo:lo + cw])
            b = _dot(h, w_up_ref[:, d_ff + lo:d_ff + lo + cw])
            act = (a * jax.nn.sigmoid(a) * b).astype(BF16)
            part = _dot(act, w_down_ref[lo:lo + cw, :])
            accs[t] = part if accs[t] is None else accs[t] + part
        lo += cw
    return [x + 0.5 * _rms(acc, g_post) for x, acc in zip(xs, accs)]


def _subtiles(n_tokens):
    tm = n_tokens // SUBTILES
    return [slice(t * tm, (t + 1) * tm) for t in range(SUBTILES)]


def _time_major(x):
    n, w = x.shape
    return jnp.swapaxes(x.reshape(SUBLANES, n // SUBLANES, w), 0, 1).reshape(n, w)


def _segment_major(x):
    n, w = x.shape
    return jnp.swapaxes(x.reshape(n // SUBLANES, SUBLANES, w), 0, 1).reshape(n, w)


def _band_start(j):
    return int(np.clip(j * NA_KW - NA_KW // 2, 0, GRID_W - NA_BAND))


def _ffn_proj_kernel(*refs, split):
    n_in = 1 if split is None else 2
    g1_ref, wup_ref, wdn_ref, g2_ref, gm_ref, win_ref = refs[n_in:n_in + 6]
    x1_ref, qcb_ref, kband_ref, vband_ref, xlru_ref, glru_ref, qca_ref = refs[n_in + 6:]
    toks = _subtiles(x1_ref.shape[1])
    if split is None:
        xs = [refs[0][0, tok, :] for tok in toks]
    else:
        first = pl.program_id(0) < split
        xs = [jnp.where(first, refs[0][0, tok, :], refs[1][0, tok, :]) for tok in toks]
    x1s = _swiglu_half_step(xs, g1_ref[...], wup_ref, wdn_ref, g2_ref[...])
    for tok, x1 in zip(toks, x1s):
        x1_ref[0, tok, :] = x1
    hs = [_rms(x1, gm_ref[...]).astype(BF16) for x1 in x1s]
    o = 3 * NA_W
    xlru_ref[0] = _time_major(jnp.concatenate([_dot(h, win_ref[:, o:o + LRU_W]) for h in hs], axis=0))
    for tok, h in zip(toks, hs):
        glru_ref[0, tok, :] = _dot(h, win_ref[:, o + LRU_W:o + 2 * LRU_W])
    for t, (tok, h) in enumerate(zip(toks, hs)):
        qkv = _dot(h, win_ref[:, 0:o])
        rows = qkv.shape[0] // GRID_W
        q = (qkv[:, 0:NA_W] * (NA_HEAD_DIM ** -0.5)).reshape(rows, GRID_W, NA_W)
        k = qkv[:, NA_W:2 * NA_W].reshape(rows, GRID_W, NA_W)
        v = qkv[:, 2 * NA_W:3 * NA_W].reshape(rows, GRID_W, NA_W)
        for j in range(NA_N_CB):
            bs = _band_start(j)
            qs = slice(t * rows * NA_KW, (t + 1) * rows * NA_KW)
            ks = slice(t * rows * NA_BAND, (t + 1) * rows * NA_BAND)
            qcb_ref[0, j, qs, :] = q[:, j * NA_KW:(j + 1) * NA_KW, :].reshape(rows * NA_KW, NA_W).astype(BF16)
            kband_ref[0, j, ks, :] = k[:, bs:bs + NA_BAND, :].reshape(rows * NA_BAND, NA_W).astype(BF16)
            vband_ref[0, j, ks, :] = v[:, bs:bs + NA_BAND, :].reshape(rows * NA_BAND, NA_W).astype(BF16)
        qca_ref[0, tok, :] = _dot(h, win_ref[:, o + 2 * LRU_W:o + 2 * LRU_W + CA_W]).astype(BF16)


def _resident(shape, layer):
    nd = len(shape)
    return pl.BlockSpec((None,) + tuple(shape), lambda *_: (layer,) + (0,) * nd,
                        pipeline_mode=pl.Buffered(1))


def _group_specs(block, batches, n_tiles):
    b0 = batches[0]
    return [
        pl.BlockSpec(block, lambda b, i: (jnp.minimum(b, b0 - 1), jnp.where(b < b0, i, n_tiles - 1), 0)),
        pl.BlockSpec(block, lambda b, i: (jnp.maximum(b - b0, 0), jnp.where(b >= b0, i, 0), 0)),
    ]


def _ffn_proj_call(layer, xs, g1, wup, wdn, g2, gm, win):
    T, D = xs[0].shape[1:]
    B = sum(x.shape[0] for x in xs)
    tm = TOKEN_TILE
    tok = lambda w: pl.BlockSpec((1, tm, w), lambda b, i: (b, i, 0))
    if len(xs) == 1:
        x_specs, split = [tok(D)], None
    else:
        x_specs, split = _group_specs((1, tm, D), [x.shape[0] for x in xs], T // tm), xs[0].shape[0]
    out_shape = (
        jax.ShapeDtypeStruct((B, T, D), F32),
        jax.ShapeDtypeStruct((B, NA_N_CB, T // NA_N_CB, NA_W), BF16),
        jax.ShapeDtypeStruct((B, NA_N_CB, T // 2, NA_W), BF16),
        jax.ShapeDtypeStruct((B, NA_N_CB, T // 2, NA_W), BF16),
        jax.ShapeDtypeStruct((B, T, LRU_W), F32),
        jax.ShapeDtypeStruct((B, T, LRU_W), F32),
        jax.ShapeDtypeStruct((B, T, CA_W), BF16),
    )
    out_specs = (
        tok(D),
        pl.BlockSpec((1, NA_N_CB, tm // NA_N_CB, NA_W), lambda b, i: (b, 0, i, 0)),
        pl.BlockSpec((1, NA_N_CB, tm // 2, NA_W), lambda b, i: (b, 0, i, 0)),
        pl.BlockSpec((1, NA_N_CB, tm // 2, NA_W), lambda b, i: (b, 0, i, 0)),
        tok(LRU_W), tok(LRU_W), tok(CA_W),
    )
    in_specs = x_specs + [_resident(w.shape[1:], layer) for w in (g1, wup, wdn, g2, gm, win)]
    return pl.pallas_call(
        functools.partial(_ffn_proj_kernel, split=split),
        grid=(B, T // tm),
        in_specs=in_specs,
        out_specs=out_specs,
        out_shape=out_shape,
        compiler_params=pltpu.CompilerParams(
            dimension_semantics=("arbitrary", "arbitrary"), vmem_limit_bytes=VMEM_LIMIT_BYTES),
        name="ffn_proj",
    )(*xs, g1, wup, wdn, g2, gm, win)


def _mem_kv_kernel(mem_ref, g_ref, w_ref, kv_ref):
    h = _rms(mem_ref[0], g_ref[...]).astype(BF16)
    kv_ref[0, 0] = _dot(h, w_ref[...]).astype(BF16)


def _mem_kv_call(mem, g_mem, w_kv):
    B, M, D = mem.shape
    L = w_kv.shape[0]
    return pl.pallas_call(
        _mem_kv_kernel,
        grid=(L, B),
        in_specs=[
            pl.BlockSpec((1, M, D), lambda l, b: (b, 0, 0)),
            pl.BlockSpec((None, 1, D), lambda l, b: (l, 0, 0)),
            pl.BlockSpec((None, D, 2 * CA_W), lambda l, b: (l, 0, 0)),
        ],
        out_specs=pl.BlockSpec((1, 1, M, 2 * CA_W), lambda l, b: (l, b, 0, 0)),
        out_shape=jax.ShapeDtypeStruct((L, B, M, 2 * CA_W), BF16),
        compiler_params=pltpu.CompilerParams(dimension_semantics=("arbitrary", "arbitrary")),
        name="mem_kv",
    )(mem, g_mem, w_kv)


def _na_key_row_start(i, rows):
    return jnp.clip(i * NA_QROWS - NA_KH // 2, 0, rows - NA_KROWS)


def _na_bias_tables(rpb, rows):
    L, H = rpb.shape[:2]
    n_blocks = rows // NA_QROWS
    n_dr = 2 * NA_KH - 1
    j = np.arange(NA_N_CB)[:, None, None]
    qc = j * NA_KW + np.arange(NA_KW)[None, :, None]
    bs = np.clip(j * NA_KW - NA_KW // 2, 0, GRID_W - NA_BAND)
    kc = bs + np.arange(NA_BAND)[None, None, :]
    ws = np.clip(qc - NA_KW // 2, 0, GRID_W - NA_KW)
    col_ok = (kc >= ws) & (kc < ws + NA_KW)
    rp = jnp.pad(rpb, ((0, 0), (0, 0), (0, 0), (NA_BAND, NA_BAND)))
    cols = []
    for jj in range(NA_N_CB):
        for q in range(NA_KW):
            shift = NA_BAND + int(bs[jj, 0, 0]) - (jj * NA_KW + q) + (NA_KW - 1)
            cols.append(rp[..., shift:shift + NA_BAND])
    t1 = jnp.stack(cols).reshape(NA_N_CB, NA_KW, L, H, n_dr, NA_BAND)
    t1 = jnp.where(jnp.asarray(col_ok)[:, :, None, None, None, :], t1, NEG)
    t1 = t1.reshape(NA_N_CB, NA_KW, L, H, n_dr * NA_BAND)
    runs = []
    for i in (0, min(1, n_blocks - 1), n_blocks - 1):
        ks = int(np.clip(i * NA_QROWS - NA_KH // 2, 0, rows - NA_KROWS))
        for aa in range(NA_QROWS):
            r = i * NA_QROWS + aa
            rs = int(np.clip(r - NA_KH // 2, 0, rows - NA_KH))
            lo = rs - ks
            dr0 = rs - r + NA_KH - 1
            assert 0 <= lo <= NA_KROWS - NA_KH and 0 <= dr0 <= n_dr - NA_KH
            run = t1[..., dr0 * NA_BAND:(dr0 + NA_KH) * NA_BAND]
            runs.append(jnp.pad(run, ((0, 0),) * 4 + ((lo * NA_BAND, (NA_KROWS - NA_KH - lo) * NA_BAND),),
                                constant_values=NEG))
    nk = NA_KROWS * NA_BAND
    bias = jnp.stack(runs).reshape(3, NA_QROWS, NA_N_CB, NA_KW, L, H, nk)
    bias = jnp.transpose(bias, (4, 0, 5, 2, 1, 3, 6))
    return bias.reshape(L, 3, H, NA_N_CB, NA_QROWS * NA_KW, nk)


def _na_kernel(q_ref, k_ref, v_ref, bias_ref, o_ref, *, rows):
    step = pl.program_id(2)
    n_blocks = rows // NA_QROWS
    nq = NA_QROWS * NA_KW
    nk = NA_KROWS * NA_BAND
    lane = lax.broadcasted_iota(jnp.int32, (nq, LANES), 1)
    first_head = lane < NA_HEAD_DIM
    units = [(sub, j) for sub in range(NA_BLOCKS_PER_STEP) for j in range(NA_N_CB)]

    def scores(sub, j):
        i = step * NA_BLOCKS_PER_STEP + sub
        ks = pl.multiple_of(_na_key_row_start(i, rows) * NA_BAND, NA_BAND * (NA_KH // 2))
        variant = jnp.where(i == 0, 0, jnp.where(i == n_blocks - 1, 2, 1))
        q = q_ref[0, j, sub * nq:(sub + 1) * nq, :]
        k = k_ref[0, j, pl.ds(ks, nk), :]
        zero = jnp.zeros_like(q)
        q2 = jnp.concatenate([jnp.where(first_head, q, zero), jnp.where(first_head, zero, q)], axis=0)
        return _dot_nt(q2, k) + bias_ref[variant, :, j].reshape(NA_HEADS_PER_STEP * nq, nk), ks

    ones = jnp.ones((nk, LANES), BF16)

    def softmax(s):
        m = jnp.max(s, axis=-1, keepdims=True)
        return jnp.exp(s - m).astype(BF16)

    def finish(sub, j, p, ks):
        v = v_ref[0, j, pl.ds(ks, nk), :]
        ov = _dot(p, jnp.concatenate([v, ones], axis=1))
        o2 = ov[:, :LANES] / ov[:, LANES:]
        o = jnp.where(first_head, o2[:nq], o2[nq:]).astype(BF16)
        for a in range(NA_QROWS):
            row0 = (sub * NA_QROWS + a) * GRID_W + j * NA_KW
            o_ref[0, row0:row0 + NA_KW, :] = o[a * NA_KW:(a + 1) * NA_KW, :]

    n = len(units)
    sc, pr = {}, {}
    for t in range(n + 2):
        if t < n:
            sc[t] = scores(*units[t])
        if 0 <= t - 1 < n:
            s, ks = sc.pop(t - 1)
            pr[t - 1] = (softmax(s), ks)
        if 0 <= t - 2 < n:
            finish(*units[t - 2], *pr.pop(t - 2))


def _na_call(layer, q_cb, k_band, v_band, bias):
    B, _, tq, _ = q_cb.shape
    T = tq * NA_N_CB
    rows = T // GRID_W
    n_steps = rows // (NA_QROWS * NA_BLOCKS_PER_STEP)
    n_pairs = NA_HEADS // NA_HEADS_PER_STEP
    nq = NA_QROWS * NA_KW
    nk = NA_KROWS * NA_BAND
    kv_spec = pl.BlockSpec((1, NA_N_CB, T // 2, LANES), lambda hp, b, i: (b, 0, 0, hp))
    return pl.pallas_call(
        functools.partial(_na_kernel, rows=rows),
        grid=(n_pairs, B, n_steps),
        in_specs=[
            pl.BlockSpec((1, NA_N_CB, nq * NA_BLOCKS_PER_STEP, LANES), lambda hp, b, i: (b, 0, i, hp)),
            kv_spec, kv_spec,
            pl.BlockSpec((None, 3, NA_HEADS_PER_STEP, NA_N_CB, nq, nk),
                         lambda hp, b, i: (layer, 0, hp, 0, 0, 0), pipeline_mode=pl.Buffered(1)),
        ],
        out_specs=pl.BlockSpec((1, NA_QROWS * NA_BLOCKS_PER_STEP * GRID_W, LANES), lambda hp, b, i: (b, i, hp)),
        out_shape=jax.ShapeDtypeStruct((B, T, NA_W), BF16),
        compiler_params=pltpu.CompilerParams(
            dimension_semantics=("arbitrary", "arbitrary", "arbitrary"), vmem_limit_bytes=VMEM_LIMIT_BYTES),
        name="na_attn",
    )(q_cb, k_band, v_band, bias)


def _lru_gate_weights(wa, wi, ba, bi):
    L = wa.shape[0]
    ng = LRU_BLOCKS // LRU_GROUP

    def blockdiag(w):
        w = w.reshape(L, 2, ng, LRU_GROUP, LRU_BW, LRU_BW)
        eye = jnp.eye(LRU_GROUP, dtype=w.dtype)
        full = w[:, :, :, :, :, None, :] * eye[:, None, :, None]
        return full.reshape(L, 2, ng, LANES, LANES)

    w = jnp.concatenate([blockdiag(wa), blockdiag(wi)], axis=-1).astype(BF16)
    b = jnp.concatenate([ba.reshape(L, 2, ng, 1, LANES), bi.reshape(L, 2, ng, 1, LANES)], axis=-1)
    b_head = b.astype(BF16)
    b_rest = (b - b_head.astype(F32)).astype(BF16)
    zeros = jnp.zeros((L, 2, ng, LANES - 2, 2 * LANES), BF16)
    return jnp.concatenate([w, b_head, b_rest, zeros], axis=-2)


def _softplus(x):
    return jnp.maximum(x, 0.0) + jnp.log1p(jnp.exp(-jnp.abs(x)))


def _lru_inputs(prev_ref, x_ref, next_ref, has_prev, has_next, cw_ref, cb_ref, w_ref, lam_ref, d,
                a_scr, u_scr):
    assert CONV_W == 4
    ct, w = x_ref.shape[1], x_ref.shape[2]
    nk = ct // SUBLANES
    x = x_ref[0].reshape(nk, SUBLANES, w)
    sub = lax.broadcasted_iota(jnp.int32, (SUBLANES, w), 0)
    prev = jnp.where(has_prev, prev_ref[0], 0.0)
    nxt = jnp.where(has_next, next_ref[0], 0.0)
    back1 = jnp.where(sub == 0, pltpu.roll(prev[SUBLANES:], 1, 0), pltpu.roll(x[nk - 1], 1, 0))
    back2 = jnp.where(sub == 0, pltpu.roll(prev[:SUBLANES], 1, 0), pltpu.roll(x[nk - 2], 1, 0))
    fwd1 = jnp.where(sub == SUBLANES - 1, pltpu.roll(nxt, SUBLANES - 1, 0), pltpu.roll(x[0], SUBLANES - 1, 0))
    taps = (
        jnp.concatenate([back2[None], back1[None], x[:nk - 2]], axis=0),
        jnp.concatenate([back1[None], x[:nk - 1]], axis=0),
        x,
        jnp.concatenate([x[1:], fwd1[None]], axis=0),
    )
    xc = cb_ref[...]
    for t in range(CONV_W):
        xc = xc + taps[t] * cw_ref[t:t + 1, :]
    xc = xc.reshape(ct, w)
    xcb = xc.astype(BF16)
    sp_c = LRU_C * _softplus(-lam_ref[d:d + 1, :])
    ones = jnp.ones((ct, LANES), BF16)
    for g in range(LRU_W // LANES):
        sl = slice(g * LANES, (g + 1) * LANES)
        z = _dot(jnp.concatenate([xcb[:, sl], ones], axis=1), w_ref[d, g])
        r = jax.nn.sigmoid(z[:, :LANES])
        gi = jax.nn.sigmoid(z[:, LANES:])
        neg_log_a = r * sp_c[:, sl]
        a = jnp.exp(-neg_log_a)
        y = jnp.tanh(neg_log_a) * (a * a + 1.0)
        u = (y * lax.rsqrt(jnp.maximum(y, F32_TINY))) * (gi * xc[:, sl])
        a_scr[:, sl] = a
        u_scr[:, sl] = u


def _slab_scan(a, b, reverse):
    row = lax.broadcasted_iota(jnp.int32, a.shape, 0)
    for sh in (1, 2, 4):
        if reverse:
            ok = row < SUBLANES - sh
            a_sh = pltpu.roll(a, SUBLANES - sh, 0)
            b_sh = pltpu.roll(b, SUBLANES - sh, 0)
        else:
            ok = row >= sh
            a_sh = pltpu.roll(a, sh, 0)
            b_sh = pltpu.roll(b, sh, 0)
        b = a * jnp.where(ok, b_sh, 0.0) + b
        a = a * jnp.where(ok, a_sh, 1.0)
    return a, b


def _segment_scan(a_scr, u_scr, h_ref, carry_scr, reverse):
    ct, w = a_scr.shape
    nk = ct // SUBLANES

    def group(k):
        kk = (nk - 1 - k) if reverse else k
        return slice(kk * SUBLANES, (kk + 1) * SUBLANES)

    h = jnp.zeros((SUBLANES, w), F32)
    acc = jnp.ones((SUBLANES, w), F32)
    for k in range(nk):
        rows = group(k)
        a = a_scr[rows, :]
        h = a * h + u_scr[rows, :]
        acc = a * acc
        h_ref[0, rows, :] = h
        a_scr[rows, :] = acc
    a_inc, h_inc = _slab_scan(acc, h, reverse)
    state = h_inc + a_inc * carry_scr[...]
    sub = lax.broadcasted_iota(jnp.int32, (SUBLANES, w), 0)
    if reverse:
        carry_out = state[0:1, :]
        incoming = jnp.where(sub == SUBLANES - 1, carry_scr[...], pltpu.roll(state, SUBLANES - 1, 0))
    else:
        carry_out = state[SUBLANES - 1:SUBLANES, :]
        incoming = jnp.where(sub == 0, carry_scr[...], pltpu.roll(state, 1, 0))
    carry_scr[...] = carry_out
    for k in range(nk):
        rows = group(k)
        h_ref[0, rows, :] = h_ref[0, rows, :] + a_scr[rows, :] * incoming


def _lru_kernel(xf_ref, pf_ref, nf_ref, xb_ref, pb_ref, nb_ref, cw_ref, cb_ref, w_ref, lam_ref,
                hf_ref, hb_ref, af_scr, uf_scr, ab_scr, ub_scr, cf_scr, cbk_scr):
    c = pl.program_id(1)
    nc = pl.num_programs(1)

    @pl.when(c == 0)
    def _():
        cf_scr[...] = jnp.zeros_like(cf_scr)
        cbk_scr[...] = jnp.zeros_like(cbk_scr)

    _lru_inputs(pf_ref, xf_ref, nf_ref, c > 0, c < nc - 1, cw_ref, cb_ref, w_ref, lam_ref, 0,
                af_scr, uf_scr)
    _lru_inputs(pb_ref, xb_ref, nb_ref, c < nc - 1, c > 0, cw_ref, cb_ref, w_ref, lam_ref, 1,
                ab_scr, ub_scr)
    _segment_scan(af_scr, uf_scr, hf_ref, cf_scr, False)
    _segment_scan(ab_scr, ub_scr, hb_ref, cbk_scr, True)


def _lru_call(layer, x_lru, conv_w, conv_b, w_gate, lam):
    B, T, W = x_lru.shape
    ct = LRU_CHUNK
    nc = T // ct
    tail = 2 * SUBLANES
    n_tail = ct // tail
    n_head = ct // SUBLANES

    main_f = pl.BlockSpec((1, ct, W), lambda b, c: (b, c, 0))
    prev_f = pl.BlockSpec((1, tail, W), lambda b, c: (b, jnp.maximum(c * n_tail - 1, 0), 0))
    next_f = pl.BlockSpec((1, SUBLANES, W), lambda b, c: (b, jnp.minimum(c + 1, nc - 1) * n_head, 0))
    main_b = pl.BlockSpec((1, ct, W), lambda b, c: (b, nc - 1 - c, 0))
    prev_b = pl.BlockSpec((1, tail, W), lambda b, c: (b, jnp.maximum((nc - 1 - c) * n_tail - 1, 0), 0))
    next_b = pl.BlockSpec((1, SUBLANES, W), lambda b, c: (b, jnp.minimum(nc - c, nc - 1) * n_head, 0))
    params = (conv_w, conv_b, w_gate, lam)
    return pl.pallas_call(
        _lru_kernel,
        grid=(B, nc),
        in_specs=[main_f, prev_f, next_f, main_b, prev_b, next_b] + [_resident(p.shape[1:], layer) for p in params],
        out_specs=(main_f, main_b),
        out_shape=(jax.ShapeDtypeStruct((B, T, W), F32), jax.ShapeDtypeStruct((B, T, W), F32)),
        scratch_shapes=[pltpu.VMEM((ct, W), F32)] * 4 + [pltpu.VMEM((1, W), F32)] * 2,
        compiler_params=pltpu.CompilerParams(dimension_semantics=("arbitrary", "arbitrary")),
        name="rg_lru",
    )(x_lru, x_lru, x_lru, x_lru, x_lru, x_lru, *params)


def _merge_ffn_kernel(x_ref, yna_ref, hf_ref, hb_ref, glru_ref, qca_ref, kv_ref,
                      gm_ref, wg_ref, bg_ref, wna_ref, wlru_ref, wca_ref, wout_ref, gpost_ref,
                      g1_ref, wup_ref, wdn_ref, g2_ref, *o_refs, split):
    toks = _subtiles(x_ref.shape[1])
    d = x_ref.shape[-1]
    scale = CA_HEAD_DIM ** -0.5
    scores = [[_dot_nt(qca_ref[0, tok, hd * CA_HEAD_DIM:(hd + 1) * CA_HEAD_DIM],
                       kv_ref[0, 0, :, hd * CA_HEAD_DIM:(hd + 1) * CA_HEAD_DIM]) * scale for tok in toks]
              for hd in range(CA_HEADS)]
    x1s = [x_ref[0, tok, :] for tok in toks]
    hs = [_rms(x1, gm_ref[...]).astype(BF16) for x1 in x1s]

    heads = [[] for _ in toks]
    ca_ones = jnp.ones((kv_ref.shape[2], CA_HEAD_DIM), BF16)
    for hd in range(CA_HEADS):
        v = kv_ref[0, 0, :, CA_W + hd * CA_HEAD_DIM:CA_W + (hd + 1) * CA_HEAD_DIM]
        for t in range(len(toks)):
            s = scores[hd][t]
            m = jnp.max(s, axis=-1, keepdims=True)
            p = jnp.exp(s - m).astype(BF16)
            ov = _dot(p, jnp.concatenate([v, ca_ones], axis=1))
            heads[t].append((ov[:, :CA_HEAD_DIM] / ov[:, CA_HEAD_DIM:]).astype(BF16))
    y_cas = [jnp.concatenate(hh, axis=-1) for hh in heads]

    def gate(h, n):
        return jax.nn.sigmoid(_dot(h, wg_ref[:, n * d:(n + 1) * d]) + bg_ref[:, n * d:(n + 1) * d])

    merged = [gate(h, 0) * _dot(yna_ref[0, tok, :], wna_ref[...]) for tok, h in zip(toks, hs)]
    h_lru = _segment_major(hf_ref[0] + hb_ref[0])
    merged = [mg + gate(h, 1) * _dot((h_lru[tok, :] * jax.nn.gelu(glru_ref[0, tok, :])).astype(BF16), wlru_ref[...])
              for tok, h, mg in zip(toks, hs, merged)]
    merged = [mg + gate(h, 2) * _dot(y_ca, wca_ref[...]) for h, mg, y_ca in zip(hs, merged, y_cas)]
    x2s = [x1 + _rms(_dot(mg.astype(BF16), wout_ref[...]), gpost_ref[...]) for x1, mg in zip(x1s, merged)]
    outs = _swiglu_half_step(x2s, g1_ref[...], wup_ref, wdn_ref, g2_ref[...])

    def store(o_ref):
        for tok, out in zip(toks, outs):
            o_ref[0, tok, :] = out

    if split is None:
        store(o_refs[0])
    else:
        first = pl.program_id(0) < split
        pl.when(first)(lambda: store(o_refs[0]))
        pl.when(jnp.logical_not(first))(lambda: store(o_refs[1]))


def _merge_ffn_call(layer, x1, y_na, h_f, h_b, g_lru, q_ca, kv, params, out_batches=None):
    B, T, D = x1.shape
    M = kv.shape[2]
    tm = TOKEN_TILE
    tok = lambda w: pl.BlockSpec((1, tm, w), lambda b, i: (b, i, 0))
    in_specs = [tok(D), tok(NA_W), tok(LRU_W), tok(LRU_W), tok(LRU_W), tok(CA_W),
                pl.BlockSpec((1, 1, M, 2 * CA_W), lambda b, i: (layer, b, 0, 0))]
    in_specs += [_resident(p.shape[1:], layer) for p in params]
    if out_batches is None:
        out_specs, out_shape, split = tok(D), jax.ShapeDtypeStruct((B, T, D), F32), None
    else:
        assert sum(out_batches) == B
        out_specs = tuple(_group_specs((1, tm, D), out_batches, T // tm))
        out_shape = tuple(jax.ShapeDtypeStruct((n, T, D), F32) for n in out_batches)
        split = out_batches[0]
    return pl.pallas_call(
        functools.partial(_merge_ffn_kernel, split=split),
        grid=(B, T // tm),
        in_specs=in_specs,
        out_specs=out_specs,
        out_shape=out_shape,
        compiler_params=pltpu.CompilerParams(
            dimension_semantics=("arbitrary", "arbitrary"), vmem_limit_bytes=VMEM_LIMIT_BYTES),
        name="merge_ffn",
    )(x1, y_na, h_f, h_b, g_lru, q_ca, kv, *params)


def kernel(x_prompt, x_sample, mem_prompt, mem_sample, g_ffn1_pre, w_ffn1_up, w_ffn1_down, g_ffn1_post, g_mix_pre, w_in, na_rpb, conv_w, conv_b, lru_wa, lru_ba, lru_wi, lru_bi, lru_lambda, g_mem, w_mem_kv, w_gate, b_gate, w_branch_na, w_branch_lru, w_branch_ca, w_out, g_mix_post, g_ffn2_pre, w_ffn2_up, w_ffn2_down, g_ffn2_post):
    assert x_prompt.shape[1:] == x_sample.shape[1:] and mem_prompt.shape[1:] == mem_sample.shape[1:]
    batches = (x_prompt.shape[0], x_sample.shape[0])
    mem = jnp.concatenate([mem_prompt, mem_sample], axis=0)
    T = x_prompt.shape[1]
    L = w_in.shape[0]
    rows = T // GRID_W
    assert rows % (NA_QROWS * NA_BLOCKS_PER_STEP) == 0 and rows >= NA_KROWS
    assert T % TOKEN_TILE == 0 and LRU_CHUNK == TOKEN_TILE
    assert w_ffn1_down.shape[1] % MXU_DIM == 0 and w_ffn2_down.shape[1] % MXU_DIM == 0

    bf = lambda w: w.astype(BF16)
    vec = lambda g: g.reshape(L, 1, g.shape[-1])
    w1u, w1d, w2u, w2d = bf(w_ffn1_up), bf(w_ffn1_down), bf(w_ffn2_up), bf(w_ffn2_down)
    win, wg, wout, wkv = bf(w_in), bf(w_gate), bf(w_out), bf(w_mem_kv)
    wna, wlru, wca = bf(w_branch_na), bf(w_branch_lru), bf(w_branch_ca)
    lru_w = _lru_gate_weights(lru_wa, lru_wi, lru_ba, lru_bi)
    bias = _na_bias_tables(na_rpb, rows)
    kv = _mem_kv_call(mem, vec(g_mem), wkv)

    xs = (x_prompt, x_sample)
    for l in range(L):
        x1, q_cb, k_band, v_band, x_lru, g_lru, q_ca = _ffn_proj_call(
            l, xs, vec(g_ffn1_pre), w1u, w1d, vec(g_ffn1_post), vec(g_mix_pre), win)
        y_na = _na_call(l, q_cb, k_band, v_band, bias)
        h_f, h_b = _lru_call(l, x_lru, conv_w, vec(conv_b), lru_w, lru_lambda)
        out = _merge_ffn_call(
            l, x1, y_na, h_f, h_b, g_lru, q_ca, kv,
            (vec(g_mix_pre), wg, vec(b_gate), wna, wlru, wca, wout, vec(g_mix_post),
             vec(g_ffn2_pre), w2u, w2d, vec(g_ffn2_post)),
            out_batches=batches if l == L - 1 else None)
        xs = (out,)
    return tuple(out)
```

```python
import functools

import numpy as np
import jax
import jax.numpy as jnp
from jax import lax
from jax.experimental import pallas as pl
from jax.experimental.pallas import tpu as pltpu

GRID_W = 64
NA_HEADS = 8
NA_HEAD_DIM = 64
NA_W = NA_HEADS * NA_HEAD_DIM
NA_KH = 8
NA_KW = 16
NA_N_CB = GRID_W // NA_KW
NA_BAND = 2 * NA_KW
LRU_W = 512
LRU_BLOCKS = 8
LRU_BW = LRU_W // LRU_BLOCKS
CONV_W = 4
LRU_C = 8.0
CA_HEADS = 4
CA_HEAD_DIM = 128
CA_W = CA_HEADS * CA_HEAD_DIM
EPS = 1e-6
NEG = -1e30
F32_TINY = float(np.finfo(np.float32).tiny)

LANES = 128
SUBLANES = 8
MXU_DIM = 256
VMEM_BYTES = 64 * 1024 * 1024
VMEM_LIMIT_BYTES = VMEM_BYTES - 4 * 1024 * 1024

TOKEN_TILE = 512
SUBTILES = 2
FF_CHUNK_TILES = 6
NA_QROWS = 8
NA_KROWS = 2 * NA_KH
NA_HEADS_PER_STEP = LANES // NA_HEAD_DIM
NA_BLOCKS_PER_STEP = 16
LRU_SCAN_PIECE = 16
LRU_CHUNK = 512
LRU_GROUP = LANES // LRU_BW

BF16 = jnp.bfloat16
F32 = jnp.float32


def _rms(x, g):
    return x * lax.rsqrt(jnp.mean(x * x, axis=-1, keepdims=True) + EPS) * g


def _dot(a, b):
    return jnp.dot(a, b, preferred_element_type=F32)


def _dot_nt(a, b):
    return lax.dot_general(a, b, (((1,), (1,)), ((), ())), preferred_element_type=F32)


def _ff_chunks(d_ff):
    tiles = d_ff // MXU_DIM
    widths = []
    while tiles > 0:
        n = min(FF_CHUNK_TILES, tiles)
        widths.append(n * MXU_DIM)
        tiles -= n
    return widths


def _swiglu_half_step(xs, g_pre, w_up_ref, w_down_ref, g_post):
    d_ff = w_down_ref.shape[0]
    hs = [_rms(x, g_pre).astype(BF16) for x in xs]
    accs = [None] * len(xs)
    lo = 0
    for cw in _ff_chunks(d_ff):
        for t, h in enumerate(hs):
            a = _dot(h, w_up_ref[:, lo:lo + cw])
            b = _dot(h, w_up_ref[:, d_ff + lo:d_ff + lo + cw])
            act = (a * jax.nn.sigmoid(a) * b).astype(BF16)
            part = _dot(act, w_down_ref[lo:lo + cw, :])
            accs[t] = part if accs[t] is None else accs[t] + part
        lo += cw
    return [x + 0.5 * _rms(acc, g_post) for x, acc in zip(xs, accs)]


def _subtiles(n_tokens):
    tm = n_tokens // SUBTILES
    return [slice(t * tm, (t + 1) * tm) for t in range(SUBTILES)]


def _time_major(x):
    n, w = x.shape
    return jnp.swapaxes(x.reshape(SUBLANES, n // SUBLANES, w), 0, 1).reshape(n, w)


def _segment_major(x):
    n, w = x.shape
    return jnp.swapaxes(x.reshape(n // SUBLANES, SUBLANES, w), 0, 1).reshape(n, w)


def _band_start(j):
    return int(np.clip(j * NA_KW - NA_KW // 2, 0, GRID_W - NA_BAND))


def _ffn_proj_kernel(*refs, split):
    n_in = 1 if split is None else 2
    g1_ref, wup_ref, wdn_ref, g2_ref, gm_ref, win_ref = refs[n_in:n_in + 6]
    x1_ref, qcb_ref, kband_ref, vband_ref, xlru_ref, glru_ref, qca_ref = refs[n_in + 6:]
    toks = _subtiles(x1_ref.shape[1])
    if split is None:
        xs = [refs[0][0, tok, :] for tok in toks]
    else:
        first = pl.program_id(0) < split
        xs = [jnp.where(first, refs[0][0, tok, :], refs[1][0, tok, :]) for tok in toks]
    x1s = _swiglu_half_step(xs, g1_ref[...], wup_ref, wdn_ref, g2_ref[...])
    for tok, x1 in zip(toks, x1s):
        x1_ref[0, tok, :] = x1
    hs = [_rms(x1, gm_ref[...]).astype(BF16) for x1 in x1s]
    o = 3 * NA_W
    xlru_ref[0] = _time_major(jnp.concatenate([_dot(h, win_ref[:, o:o + LRU_W]) for h in hs], axis=0))
    for tok, h in zip(toks, hs):
        glru_ref[0, tok, :] = jax.nn.gelu(_dot(h, win_ref[:, o + LRU_W:o + 2 * LRU_W]))
    for t, (tok, h) in enumerate(zip(toks, hs)):
        qkv = _dot(h, win_ref[:, 0:o])
        rows = qkv.shape[0] // GRID_W
        q = (qkv[:, 0:NA_W] * (NA_HEAD_DIM ** -0.5)).reshape(rows, GRID_W, NA_W)
        k = qkv[:, NA_W:2 * NA_W].reshape(rows, GRID_W, NA_W)
        v = qkv[:, 2 * NA_W:3 * NA_W].reshape(rows, GRID_W, NA_W)
        for j in range(NA_N_CB):
            bs = _band_start(j)
            qs = slice(t * rows * NA_KW, (t + 1) * rows * NA_KW)
            ks = slice(t * rows * NA_BAND, (t + 1) * rows * NA_BAND)
            qcb_ref[0, j, qs, :] = q[:, j * NA_KW:(j + 1) * NA_KW, :].reshape(rows * NA_KW, NA_W).astype(BF16)
            kband_ref[0, j, ks, :] = k[:, bs:bs + NA_BAND, :].reshape(rows * NA_BAND, NA_W).astype(BF16)
            vband_ref[0, j, ks, :] = v[:, bs:bs + NA_BAND, :].reshape(rows * NA_BAND, NA_W).astype(BF16)
        qca_ref[0, tok, :] = _dot(h, win_ref[:, o + 2 * LRU_W:o + 2 * LRU_W + CA_W]).astype(BF16)


def _resident(shape, layer):
    nd = len(shape)
    return pl.BlockSpec((None,) + tuple(shape), lambda *_: (layer,) + (0,) * nd,
                        pipeline_mode=pl.Buffered(1))


def _group_specs(block, batches, n_tiles):
    b0 = batches[0]
    return [
        pl.BlockSpec(block, lambda b, i: (jnp.minimum(b, b0 - 1), jnp.where(b < b0, i, n_tiles - 1), 0)),
        pl.BlockSpec(block, lambda b, i: (jnp.maximum(b - b0, 0), jnp.where(b >= b0, i, 0), 0)),
    ]


def _ffn_proj_call(layer, xs, g1, wup, wdn, g2, gm, win):
    T, D = xs[0].shape[1:]
    B = sum(x.shape[0] for x in xs)
    tm = TOKEN_TILE
    tok = lambda w: pl.BlockSpec((1, tm, w), lambda b, i: (b, i, 0))
    if len(xs) == 1:
        x_specs, split = [tok(D)], None
    else:
        x_specs, split = _group_specs((1, tm, D), [x.shape[0] for x in xs], T // tm), xs[0].shape[0]
    out_shape = (
        jax.ShapeDtypeStruct((B, T, D), F32),
        jax.ShapeDtypeStruct((B, NA_N_CB, T // NA_N_CB, NA_W), BF16),
        jax.ShapeDtypeStruct((B, NA_N_CB, T // 2, NA_W), BF16),
        jax.ShapeDtypeStruct((B, NA_N_CB, T // 2, NA_W), BF16),
        jax.ShapeDtypeStruct((B, T, LRU_W), F32),
        jax.ShapeDtypeStruct((B, T, LRU_W), F32),
        jax.ShapeDtypeStruct((B, T, CA_W), BF16),
    )
    out_specs = (
        tok(D),
        pl.BlockSpec((1, NA_N_CB, tm // NA_N_CB, NA_W), lambda b, i: (b, 0, i, 0)),
        pl.BlockSpec((1, NA_N_CB, tm // 2, NA_W), lambda b, i: (b, 0, i, 0)),
        pl.BlockSpec((1, NA_N_CB, tm // 2, NA_W), lambda b, i: (b, 0, i, 0)),
        tok(LRU_W), tok(LRU_W), tok(CA_W),
    )
    in_specs = x_specs + [_resident(w.shape[1:], layer) for w in (g1, wup, wdn, g2, gm, win)]
    return pl.pallas_call(
        functools.partial(_ffn_proj_kernel, split=split),
        grid=(B, T // tm),
        in_specs=in_specs,
        out_specs=out_specs,
        out_shape=out_shape,
        compiler_params=pltpu.CompilerParams(
            dimension_semantics=("arbitrary", "arbitrary"), vmem_limit_bytes=VMEM_LIMIT_BYTES),
        name="ffn_proj",
    )(*xs, g1, wup, wdn, g2, gm, win)


def _mem_kv_kernel(mem_ref, g_ref, w_ref, kv_ref):
    h = _rms(mem_ref[0], g_ref[...]).astype(BF16)
    kv_ref[0, 0] = _dot(h, w_ref[...]).astype(BF16)


def _mem_kv_call(mem, g_mem, w_kv):
    B, M, D = mem.shape
    L = w_kv.shape[0]
    return pl.pallas_call(
        _mem_kv_kernel,
        grid=(L, B),
        in_specs=[
            pl.BlockSpec((1, M, D), lambda l, b: (b, 0, 0)),
            pl.BlockSpec((None, 1, D), lambda l, b: (l, 0, 0)),
            pl.BlockSpec((None, D, 2 * CA_W), lambda l, b: (l, 0, 0)),
        ],
        out_specs=pl.BlockSpec((1, 1, M, 2 * CA_W), lambda l, b: (l, b, 0, 0)),
        out_shape=jax.ShapeDtypeStruct((L, B, M, 2 * CA_W), BF16),
        compiler_params=pltpu.CompilerParams(dimension_semantics=("arbitrary", "arbitrary")),
        name="mem_kv",
    )(mem, g_mem, w_kv)


def _na_key_row_start(i, rows):
    return jnp.clip(i * NA_QROWS - NA_KH // 2, 0, rows - NA_KROWS)


def _na_bias_tables(rpb, rows):
    L, H = rpb.shape[:2]
    n_blocks = rows // NA_QROWS
    n_dr = 2 * NA_KH - 1
    j = np.arange(NA_N_CB)[:, None, None]
    qc = j * NA_KW + np.arange(NA_KW)[None, :, None]
    bs = np.clip(j * NA_KW - NA_KW // 2, 0, GRID_W - NA_BAND)
    kc = bs + np.arange(NA_BAND)[None, None, :]
    ws = np.clip(qc - NA_KW // 2, 0, GRID_W - NA_KW)
    col_ok = (kc >= ws) & (kc < ws + NA_KW)
    rp = jnp.pad(rpb, ((0, 0), (0, 0), (0, 0), (NA_BAND, NA_BAND)))
    cols = []
    for jj in range(NA_N_CB):
        for q in range(NA_KW):
            shift = NA_BAND + int(bs[jj, 0, 0]) - (jj * NA_KW + q) + (NA_KW - 1)
            cols.append(rp[..., shift:shift + NA_BAND])
    t1 = jnp.stack(cols).reshape(NA_N_CB, NA_KW, L, H, n_dr, NA_BAND)
    t1 = jnp.where(jnp.asarray(col_ok)[:, :, None, None, None, :], t1, NEG)
    t1 = t1.reshape(NA_N_CB, NA_KW, L, H, n_dr * NA_BAND)
    runs = []
    for i in (0, min(1, n_blocks - 1), n_blocks - 1):
        ks = int(np.clip(i * NA_QROWS - NA_KH // 2, 0, rows - NA_KROWS))
        for aa in range(NA_QROWS):
            r = i * NA_QROWS + aa
            rs = int(np.clip(r - NA_KH // 2, 0, rows - NA_KH))
            lo = rs - ks
            dr0 = rs - r + NA_KH - 1
            assert 0 <= lo <= NA_KROWS - NA_KH and 0 <= dr0 <= n_dr - NA_KH
            run = t1[..., dr0 * NA_BAND:(dr0 + NA_KH) * NA_BAND]
            runs.append(jnp.pad(run, ((0, 0),) * 4 + ((lo * NA_BAND, (NA_KROWS - NA_KH - lo) * NA_BAND),),
                                constant_values=NEG))
    nk = NA_KROWS * NA_BAND
    bias = jnp.stack(runs).reshape(3, NA_QROWS, NA_N_CB, NA_KW, L, H, nk)
    bias = jnp.transpose(bias, (4, 0, 5, 2, 1, 3, 6))
    return bias.reshape(L, 3, H, NA_N_CB, NA_QROWS * NA_KW, nk)


def _na_kernel(q_ref, k_ref, v_ref, bias_ref, o_ref, *, rows):
    step = pl.program_id(2)
    n_blocks = rows // NA_QROWS
    nq = NA_QROWS * NA_KW
    nk = NA_KROWS * NA_BAND
    lane = lax.broadcasted_iota(jnp.int32, (nq, LANES), 1)
    first_head = lane < NA_HEAD_DIM
    units = [(sub, j) for sub in range(NA_BLOCKS_PER_STEP) for j in range(NA_N_CB)]

    def scores(sub, j):
        i = step * NA_BLOCKS_PER_STEP + sub
        ks = pl.multiple_of(_na_key_row_start(i, rows) * NA_BAND, NA_BAND * (NA_KH // 2))
        variant = jnp.where(i == 0, 0, jnp.where(i == n_blocks - 1, 2, 1))
        q = q_ref[0, j, sub * nq:(sub + 1) * nq, :]
        k = k_ref[0, j, pl.ds(ks, nk), :]
        zero = jnp.zeros_like(q)
        q2 = jnp.concatenate([jnp.where(first_head, q, zero), jnp.where(first_head, zero, q)], axis=0)
        return _dot_nt(q2, k) + bias_ref[variant, :, j].reshape(NA_HEADS_PER_STEP * nq, nk), ks

    ones = jnp.ones((nk, LANES), BF16)

    def softmax(s):
        m = jnp.max(s, axis=-1, keepdims=True)
        return jnp.exp(s - m).astype(BF16)

    def finish(sub, j, p, ks):
        v = v_ref[0, j, pl.ds(ks, nk), :]
        ov = _dot(p, jnp.concatenate([v, ones], axis=1))
        o2 = ov[:, :LANES] / ov[:, LANES:]
        o = jnp.where(first_head, o2[:nq], o2[nq:]).astype(BF16)
        for a in range(NA_QROWS):
            row0 = (sub * NA_QROWS + a) * GRID_W + j * NA_KW
            o_ref[0, row0:row0 + NA_KW, :] = o[a * NA_KW:(a + 1) * NA_KW, :]

    n = len(units)
    sc, pr = {}, {}
    for t in range(n + 2):
        if t < n:
            sc[t] = scores(*units[t])
        if 0 <= t - 1 < n:
            s, ks = sc.pop(t - 1)
            pr[t - 1] = (softmax(s), ks)
        if 0 <= t - 2 < n:
            finish(*units[t - 2], *pr.pop(t - 2))


def _na_call(layer, q_cb, k_band, v_band, bias):
    B, _, tq, _ = q_cb.shape
    T = tq * NA_N_CB
    rows = T // GRID_W
    n_steps = rows // (NA_QROWS * NA_BLOCKS_PER_STEP)
    n_pairs = NA_HEADS // NA_HEADS_PER_STEP
    nq = NA_QROWS * NA_KW
    nk = NA_KROWS * NA_BAND
    kv_spec = pl.BlockSpec((1, NA_N_CB, T // 2, LANES), lambda hp, b, i: (b, 0, 0, hp))
    return pl.pallas_call(
        functools.partial(_na_kernel, rows=rows),
        grid=(n_pairs, B, n_steps),
        in_specs=[
            pl.BlockSpec((1, NA_N_CB, nq * NA_BLOCKS_PER_STEP, LANES), lambda hp, b, i: (b, 0, i, hp)),
            kv_spec, kv_spec,
            pl.BlockSpec((None, 3, NA_HEADS_PER_STEP, NA_N_CB, nq, nk),
                         lambda hp, b, i: (layer, 0, hp, 0, 0, 0), pipeline_mode=pl.Buffered(1)),
        ],
        out_specs=pl.BlockSpec((1, NA_QROWS * NA_BLOCKS_PER_STEP * GRID_W, LANES), lambda hp, b, i: (b, i, hp)),
        out_shape=jax.ShapeDtypeStruct((B, T, NA_W), BF16),
        compiler_params=pltpu.CompilerParams(
            dimension_semantics=("arbitrary", "arbitrary", "arbitrary"), vmem_limit_bytes=VMEM_LIMIT_BYTES),
        name="na_attn",
    )(q_cb, k_band, v_band, bias)


def _lru_gate_weights(wa, wi, ba, bi):
    L = wa.shape[0]
    ng = LRU_BLOCKS // LRU_GROUP

    def blockdiag(w):
        w = w.reshape(L, 2, ng, LRU_GROUP, LRU_BW, LRU_BW)
        eye = jnp.eye(LRU_GROUP, dtype=w.dtype)
        full = w[:, :, :, :, :, None, :] * eye[:, None, :, None]
        return full.reshape(L, 2, ng, LANES, LANES)

    w = jnp.concatenate([blockdiag(wa), blockdiag(wi)], axis=-1).astype(BF16)
    b = jnp.concatenate([ba.reshape(L, 2, ng, 1, LANES), bi.reshape(L, 2, ng, 1, LANES)], axis=-1)
    b_head = b.astype(BF16)
    b_rest = (b - b_head.astype(F32)).astype(BF16)
    zeros = jnp.zeros((L, 2, ng, LANES - 2, 2 * LANES), BF16)
    return jnp.concatenate([w, b_head, b_rest, zeros], axis=-2)


def _softplus(x):
    return jnp.maximum(x, 0.0) + jnp.log1p(jnp.exp(-jnp.abs(x)))


def _lru_inputs(prev_ref, x_ref, next_ref, has_prev, has_next, cw_ref, cb_ref, w_ref, lam_ref, d,
                a_scr, u_scr):
    assert CONV_W == 4
    ct, w = x_ref.shape[1], x_ref.shape[2]
    nk = ct // SUBLANES
    x = x_ref[0].reshape(nk, SUBLANES, w)
    sub = lax.broadcasted_iota(jnp.int32, (SUBLANES, w), 0)
    prev = jnp.where(has_prev, prev_ref[0], 0.0)
    nxt = jnp.where(has_next, next_ref[0], 0.0)
    back1 = jnp.where(sub == 0, pltpu.roll(prev[SUBLANES:], 1, 0), pltpu.roll(x[nk - 1], 1, 0))
    back2 = jnp.where(sub == 0, pltpu.roll(prev[:SUBLANES], 1, 0), pltpu.roll(x[nk - 2], 1, 0))
    fwd1 = jnp.where(sub == SUBLANES - 1, pltpu.roll(nxt, SUBLANES - 1, 0), pltpu.roll(x[0], SUBLANES - 1, 0))
    taps = (
        jnp.concatenate([back2[None], back1[None], x[:nk - 2]], axis=0),
        jnp.concatenate([back1[None], x[:nk - 1]], axis=0),
        x,
        jnp.concatenate([x[1:], fwd1[None]], axis=0),
    )
    xc = cb_ref[...]
    for t in range(CONV_W):
        xc = xc + taps[t] * cw_ref[t:t + 1, :]
    xc = xc.reshape(ct, w)
    xcb = xc.astype(BF16)
    sp_c = LRU_C * _softplus(-lam_ref[d:d + 1, :])
    ones = jnp.ones((ct, LANES), BF16)
    for g in range(LRU_W // LANES):
        sl = slice(g * LANES, (g + 1) * LANES)
        z = _dot(jnp.concatenate([xcb[:, sl], ones], axis=1), w_ref[d, g])
        r = jax.nn.sigmoid(z[:, :LANES])
        gi = jax.nn.sigmoid(z[:, LANES:])
        neg_log_a = r * sp_c[:, sl]
        a = jnp.exp(-neg_log_a)
        y = jnp.tanh(neg_log_a) * (a * a + 1.0)
        u = (y * lax.rsqrt(jnp.maximum(y, F32_TINY))) * (gi * xc[:, sl])
        a_scr[:, sl] = a
        u_scr[:, sl] = u
        yield


def _slab_scan(a, b, reverse):
    row = lax.broadcasted_iota(jnp.int32, a.shape, 0)
    for sh in (1, 2, 4):
        if reverse:
            ok = row < SUBLANES - sh
            a_sh = pltpu.roll(a, SUBLANES - sh, 0)
            b_sh = pltpu.roll(b, SUBLANES - sh, 0)
        else:
            ok = row >= sh
            a_sh = pltpu.roll(a, sh, 0)
            b_sh = pltpu.roll(b, sh, 0)
        b = a * jnp.where(ok, b_sh, 0.0) + b
        a = a * jnp.where(ok, a_sh, 1.0)
    return a, b


def _segment_scan(a_scr, u_scr, h_ref, carry_scr, reverse):
    ct, w = a_scr.shape
    nk = ct // SUBLANES

    def group(k):
        kk = (nk - 1 - k) if reverse else k
        return slice(kk * SUBLANES, (kk + 1) * SUBLANES)

    h = jnp.zeros((SUBLANES, w), F32)
    acc = jnp.ones((SUBLANES, w), F32)
    for k in range(nk):
        rows = group(k)
        a = a_scr[rows, :]
        h = a * h + u_scr[rows, :]
        acc = a * acc
        h_ref[0, rows, :] = h
        a_scr[rows, :] = acc
        if k % LRU_SCAN_PIECE == LRU_SCAN_PIECE - 1:
            yield
    a_inc, h_inc = _slab_scan(acc, h, reverse)
    state = h_inc + a_inc * carry_scr[...]
    sub = lax.broadcasted_iota(jnp.int32, (SUBLANES, w), 0)
    if reverse:
        carry_out = state[0:1, :]
        incoming = jnp.where(sub == SUBLANES - 1, carry_scr[...], pltpu.roll(state, SUBLANES - 1, 0))
    else:
        carry_out = state[SUBLANES - 1:SUBLANES, :]
        incoming = jnp.where(sub == 0, carry_scr[...], pltpu.roll(state, 1, 0))
    carry_scr[...] = carry_out
    for k in range(nk):
        rows = group(k)
        h_ref[0, rows, :] = h_ref[0, rows, :] + a_scr[rows, :] * incoming
        if k % LRU_SCAN_PIECE == LRU_SCAN_PIECE - 1:
            yield


def _lru_kernel(xf_ref, pf_ref, nf_ref, xb_ref, pb_ref, nb_ref, cw_ref, cb_ref, w_ref, lam_ref,
                hf_ref, hb_ref, af_scr, uf_scr, ab_scr, ub_scr, cf_scr, cbk_scr):
    c = pl.program_id(1)
    nc = pl.num_programs(1)

    @pl.when(c == 0)
    def _():
        cf_scr[...] = jnp.zeros_like(cf_scr)
        cbk_scr[...] = jnp.zeros_like(cbk_scr)

    gates_f = _lru_inputs(pf_ref, xf_ref, nf_ref, c > 0, c < nc - 1, cw_ref, cb_ref, w_ref, lam_ref, 0,
                          af_scr, uf_scr)
    gates_b = _lru_inputs(pb_ref, xb_ref, nb_ref, c < nc - 1, c > 0, cw_ref, cb_ref, w_ref, lam_ref, 1,
                          ab_scr, ub_scr)
    scan_f = _segment_scan(af_scr, uf_scr, hf_ref, cf_scr, False)
    scan_b = _segment_scan(ab_scr, ub_scr, hb_ref, cbk_scr, True)
    _emit(gates_f)
    _emit(scan_f, gates_b)
    _emit(scan_b)


def _emit(*pieces):
    live = list(pieces)
    while live:
        for g in list(live):
            try:
                next(g)
            except StopIteration:
                live.remove(g)


def _lru_call(layer, x_lru, conv_w, conv_b, w_gate, lam):
    B, T, W = x_lru.shape
    ct = LRU_CHUNK
    nc = T // ct
    tail = 2 * SUBLANES
    n_tail = ct // tail
    n_head = ct // SUBLANES

    main_f = pl.BlockSpec((1, ct, W), lambda b, c: (b, c, 0))
    prev_f = pl.BlockSpec((1, tail, W), lambda b, c: (b, jnp.maximum(c * n_tail - 1, 0), 0))
    next_f = pl.BlockSpec((1, SUBLANES, W), lambda b, c: (b, jnp.minimum(c + 1, nc - 1) * n_head, 0))
    main_b = pl.BlockSpec((1, ct, W), lambda b, c: (b, nc - 1 - c, 0))
    prev_b = pl.BlockSpec((1, tail, W), lambda b, c: (b, jnp.maximum((nc - 1 - c) * n_tail - 1, 0), 0))
    next_b = pl.BlockSpec((1, SUBLANES, W), lambda b, c: (b, jnp.minimum(nc - c, nc - 1) * n_head, 0))
    params = (conv_w, conv_b, w_gate, lam)
    return pl.pallas_call(
        _lru_kernel,
        grid=(B, nc),
        in_specs=[main_f, prev_f, next_f, main_b, prev_b, next_b] + [_resident(p.shape[1:], layer) for p in params],
        out_specs=(main_f, main_b),
        out_shape=(jax.ShapeDtypeStruct((B, T, W), F32), jax.ShapeDtypeStruct((B, T, W), F32)),
        scratch_shapes=[pltpu.VMEM((ct, W), F32)] * 4 + [pltpu.VMEM((1, W), F32)] * 2,
        compiler_params=pltpu.CompilerParams(dimension_semantics=("arbitrary", "arbitrary")),
        name="rg_lru",
    )(x_lru, x_lru, x_lru, x_lru, x_lru, x_lru, *params)


def _merge_ffn_kernel(x_ref, yna_ref, hf_ref, hb_ref, glru_ref, qca_ref, kv_ref,
                      gm_ref, wg_ref, bg_ref, wna_ref, wlru_ref, wca_ref, wout_ref, gpost_ref,
                      g1_ref, wup_ref, wdn_ref, g2_ref, *o_refs, split):
    toks = _subtiles(x_ref.shape[1])
    d = x_ref.shape[-1]
    scale = CA_HEAD_DIM ** -0.5
    scores = [[_dot_nt(qca_ref[0, tok, hd * CA_HEAD_DIM:(hd + 1) * CA_HEAD_DIM],
                       kv_ref[0, 0, :, hd * CA_HEAD_DIM:(hd + 1) * CA_HEAD_DIM]) * scale for tok in toks]
              for hd in range(CA_HEADS)]
    x1s = [x_ref[0, tok, :] for tok in toks]
    hs = [_rms(x1, gm_ref[...]).astype(BF16) for x1 in x1s]

    heads = [[] for _ in toks]
    ca_ones = jnp.ones((kv_ref.shape[2], CA_HEAD_DIM), BF16)
    for hd in range(CA_HEADS):
        v = kv_ref[0, 0, :, CA_W + hd * CA_HEAD_DIM:CA_W + (hd + 1) * CA_HEAD_DIM]
        for t in range(len(toks)):
            s = scores[hd][t]
            m = jnp.max(s, axis=-1, keepdims=True)
            p = jnp.exp(s - m).astype(BF16)
            ov = _dot(p, jnp.concatenate([v, ca_ones], axis=1))
            heads[t].append((ov[:, :CA_HEAD_DIM] / ov[:, CA_HEAD_DIM:]).astype(BF16))
    y_cas = [jnp.concatenate(hh, axis=-1) for hh in heads]

    def gate(h, n):
        return jax.nn.sigmoid(_dot(h, wg_ref[:, n * d:(n + 1) * d]) + bg_ref[:, n * d:(n + 1) * d])

    merged = [gate(h, 0) * _dot(yna_ref[0, tok, :], wna_ref[...]) for tok, h in zip(toks, hs)]
    h_lru = _segment_major(hf_ref[0] + hb_ref[0])
    merged = [mg + gate(h, 1) * _dot((h_lru[tok, :] * glru_ref[0, tok, :]).astype(BF16), wlru_ref[...])
              for tok, h, mg in zip(toks, hs, merged)]
    merged = [mg + gate(h, 2) * _dot(y_ca, wca_ref[...]) for h, mg, y_ca in zip(hs, merged, y_cas)]
    x2s = [x1 + _rms(_dot(mg.astype(BF16), wout_ref[...]), gpost_ref[...]) for x1, mg in zip(x1s, merged)]
    outs = _swiglu_half_step(x2s, g1_ref[...], wup_ref, wdn_ref, g2_ref[...])

    def store(o_ref):
        for tok, out in zip(toks, outs):
            o_ref[0, tok, :] = out

    if split is None:
        store(o_refs[0])
    else:
        first = pl.program_id(0) < split
        pl.when(first)(lambda: store(o_refs[0]))
        pl.when(jnp.logical_not(first))(lambda: store(o_refs[1]))


def _merge_ffn_call(layer, x1, y_na, h_f, h_b, g_lru, q_ca, kv, params, out_batches=None):
    B, T, D = x1.shape
    M = kv.shape[2]
    tm = TOKEN_TILE
    tok = lambda w: pl.BlockSpec((1, tm, w), lambda b, i: (b, i, 0))
    in_specs = [tok(D), tok(NA_W), tok(LRU_W), tok(LRU_W), tok(LRU_W), tok(CA_W),
                pl.BlockSpec((1, 1, M, 2 * CA_W), lambda b, i: (layer, b, 0, 0))]
    in_specs += [_resident(p.shape[1:], layer) for p in params]
    if out_batches is None:
        out_specs, out_shape, split = tok(D), jax.ShapeDtypeStruct((B, T, D), F32), None
    else:
        assert sum(out_batches) == B
        out_specs = tuple(_group_specs((1, tm, D), out_batches, T // tm))
        out_shape = tuple(jax.ShapeDtypeStruct((n, T, D), F32) for n in out_batches)
        split = out_batches[0]
    return pl.pallas_call(
        functools.partial(_merge_ffn_kernel, split=split),
        grid=(B, T // tm),
        in_specs=in_specs,
        out_specs=out_specs,
        out_shape=out_shape,
        compiler_params=pltpu.CompilerParams(
            dimension_semantics=("arbitrary", "arbitrary"), vmem_limit_bytes=VMEM_LIMIT_BYTES),
        name="merge_ffn",
    )(x1, y_na, h_f, h_b, g_lru, q_ca, kv, *params)


def kernel(x_prompt, x_sample, mem_prompt, mem_sample, g_ffn1_pre, w_ffn1_up, w_ffn1_down, g_ffn1_post, g_mix_pre, w_in, na_rpb, conv_w, conv_b, lru_wa, lru_ba, lru_wi, lru_bi, lru_lambda, g_mem, w_mem_kv, w_gate, b_gate, w_branch_na, w_branch_lru, w_branch_ca, w_out, g_mix_post, g_ffn2_pre, w_ffn2_up, w_ffn2_down, g_ffn2_post):
    assert x_prompt.shape[1:] == x_sample.shape[1:] and mem_prompt.shape[1:] == mem_sample.shape[1:]
    batches = (x_prompt.shape[0], x_sample.shape[0])
    mem = jnp.concatenate([mem_prompt, mem_sample], axis=0)
    T = x_prompt.shape[1]
    L = w_in.shape[0]
    rows = T // GRID_W
    assert rows % (NA_QROWS * NA_BLOCKS_PER_STEP) == 0 and rows >= NA_KROWS
    assert T % TOKEN_TILE == 0 and LRU_CHUNK == TOKEN_TILE
    assert w_ffn1_down.shape[1] % MXU_DIM == 0 and w_ffn2_down.shape[1] % MXU_DIM == 0

    bf = lambda w: w.astype(BF16)
    vec = lambda g: g.reshape(L, 1, g.shape[-1])
    w1u, w1d, w2u, w2d = bf(w_ffn1_up), bf(w_ffn1_down), bf(w_ffn2_up), bf(w_ffn2_down)
    win, wg, wout, wkv = bf(w_in), bf(w_gate), bf(w_out), bf(w_mem_kv)
    wna, wlru, wca = bf(w_branch_na), bf(w_branch_lru), bf(w_branch_ca)
    lru_w = _lru_gate_weights(lru_wa, lru_wi, lru_ba, lru_bi)
    bias = _na_bias_tables(na_rpb, rows)
    kv = _mem_kv_call(mem, vec(g_mem), wkv)

    xs = (x_prompt, x_sample)
    for l in range(L):
        x1, q_cb, k_band, v_band, x_lru, g_lru, q_ca = _ffn_proj_call(
            l, xs, vec(g_ffn1_pre), w1u, w1d, vec(g_ffn1_post), vec(g_mix_pre), win)
        y_na = _na_call(l, q_cb, k_band, v_band, bias)
        h_f, h_b = _lru_call(l, x_lru, conv_w, vec(conv_b), lru_w, lru_lambda)
        out = _merge_ffn_call(
            l, x1, y_na, h_f, h_b, g_lru, q_ca, kv,
            (vec(g_mix_pre), wg, vec(b_gate), wna, wlru, wca, wout, vec(g_mix_post),
             vec(g_ffn2_pre), w2u, w2d, vec(g_ffn2_post)),
            out_batches=batches if l == L - 1 else None)
        xs = (out,)
    return tuple(out)
```
